```python
import jax, jax.numpy as jnp
from jax import lax
import numpy as np

D_MODEL = 1024
BATCH = 16
SEQ = 4096
DEPTH = 1
DEC_BATCH = 2
DEC_SEQ = 16384
PAST_LEN = 128

GRID_W = 64
N_HEADS = 8
N_KV_HEADS = 2
HEAD_DIM = 64
GROUP = N_HEADS // N_KV_HEADS
ATTN_W = N_HEADS * HEAD_DIM
KV_W = N_KV_HEADS * HEAD_DIM
ROPE_THETA = 10000.0
AXIS_ROT = HEAD_DIM // 2
Q_BLOCK = 128
FOURIER_GROUPS = 4
FOURIER_GROUP_W = 128
FOURIER_W = FOURIER_GROUPS * FOURIER_GROUP_W
P_IN = ATTN_W + 2 * KV_W + FOURIER_W + 2 * D_MODEL
MEM_TOKENS = 256
MEM_HEADS = 4
MEM_HEAD_DIM = D_MODEL // MEM_HEADS
MEM_W = MEM_HEADS * MEM_HEAD_DIM
D_FF = 4 * D_MODEL
ALPHA = (2 * DEPTH) ** 0.25
BETA = (8 * DEPTH) ** -0.25
RMS_EPS = 1e-6
LN_EPS = 1e-5

kernel_name = "gated_parallel_gqa_fourier_encoder"


def _layernorm(x, g, b):
    xf = x.astype(jnp.float32)
    mu = jnp.mean(xf, axis=-1, keepdims=True)
    var = jnp.mean(jnp.square(xf - mu), axis=-1, keepdims=True)
    y = (xf - mu) * lax.rsqrt(var + LN_EPS) * g.astype(jnp.float32) + b.astype(jnp.float32)
    return y.astype(x.dtype)


def _axial_rope_tables(seq_len):
    rows = seq_len // GRID_W
    row = jnp.repeat(jnp.arange(rows, dtype=jnp.float32), GRID_W)
    col = jnp.tile(jnp.arange(GRID_W, dtype=jnp.float32), rows)
    freqs = ROPE_THETA ** (-jnp.arange(0, AXIS_ROT, 2, dtype=jnp.float32) / AXIS_ROT)
    ang = jnp.concatenate([row[:, None] * freqs, col[:, None] * freqs], axis=-1)
    ang = jnp.concatenate([ang, ang], axis=-1)
    return jnp.cos(ang), jnp.sin(ang)


def _qk_prep(x, g, cos, sin):
    xf = x.astype(jnp.float32)
    xf = xf * lax.rsqrt(jnp.mean(jnp.square(xf), axis=-1, keepdims=True) + RMS_EPS) * g.astype(jnp.float32)
    half = HEAD_DIM // 2
    rot = jnp.concatenate([-xf[..., half:], xf[..., :half]], axis=-1)
    y = xf * cos[:, None, :] + rot * sin[:, None, :]
    return y.astype(x.dtype)


def _gqa_blocks(q, k, v):
    bsz, seq_len = q.shape[0], q.shape[1]
    n_blk = seq_len // Q_BLOCK
    qb = q.reshape(bsz, n_blk, Q_BLOCK, N_KV_HEADS, GROUP, HEAD_DIM).transpose(1, 0, 2, 3, 4, 5)
    scale = HEAD_DIM ** -0.5

    def one_block(q_blk):
        s = jnp.einsum('bqkgd,bskd->bkgqs', q_blk, k).astype(jnp.float32) * scale
        p = jax.nn.softmax(s, axis=-1).astype(v.dtype)
        return jnp.einsum('bkgqs,bskd->bqkgd', p, v)

    o = lax.map(one_block, qb)
    return o.transpose(1, 0, 2, 3, 4, 5).reshape(bsz, seq_len, ATTN_W)


def _fourier_mix(u):
    bsz, seq_len = u.shape[0], u.shape[1]
    ug = u.reshape(bsz, seq_len, FOURIER_GROUPS, FOURIER_GROUP_W).astype(jnp.float32)
    f = jnp.fft.fft2(ug, axes=(1, 3), norm='ortho').real
    return f.reshape(bsz, seq_len, FOURIER_W).astype(u.dtype)


def _token_mixer(x, w_in, q_norm, k_norm, w_attn_branch, w_fourier_branch, w_mix_out):
    bsz, seq_len, _ = x.shape
    h = x @ w_in
    o1 = ATTN_W
    o2 = o1 + KV_W
    o3 = o2 + KV_W
    o4 = o3 + FOURIER_W
    o5 = o4 + D_MODEL
    q = h[..., :o1].reshape(bsz, seq_len, N_HEADS, HEAD_DIM)
    k = h[..., o1:o2].reshape(bsz, seq_len, N_KV_HEADS, HEAD_DIM)
    v = h[..., o2:o3].reshape(bsz, seq_len, N_KV_HEADS, HEAD_DIM)
    u = h[..., o3:o4]
    gate_attn = h[..., o4:o5]
    gate_fourier = h[..., o5:]
    cos, sin = _axial_rope_tables(seq_len)
    q = _qk_prep(q, q_norm, cos, sin)
    k = _qk_prep(k, k_norm, cos, sin)
    y_attn = _gqa_blocks(q, k, v) @ w_attn_branch
    y_fourier = _fourier_mix(u) @ w_fourier_branch
    merged = jax.nn.sigmoid(gate_attn) * y_attn + jax.nn.sigmoid(gate_fourier) * y_fourier
    return merged @ w_mix_out


def _memory_xattn(x, mem, w_mem_q, w_mem_k, w_mem_v, w_mem_o):
    bsz, seq_len, _ = x.shape
    q = (x @ w_mem_q).reshape(bsz, seq_len, MEM_HEADS, MEM_HEAD_DIM)
    k = (mem @ w_mem_k).reshape(bsz, MEM_TOKENS, MEM_HEADS, MEM_HEAD_DIM)
    v = (mem @ w_mem_v).reshape(bsz, MEM_TOKENS, MEM_HEADS, MEM_HEAD_DIM)
    s = jnp.einsum('bqhd,bmhd->bhqm', q, k).astype(jnp.float32) * (MEM_HEAD_DIM ** -0.5)
    p = jax.nn.softmax(s, axis=-1).astype(v.dtype)
    o = jnp.einsum('bhqm,bmhd->bqhd', p, v).reshape(bsz, seq_len, MEM_W)
    return o @ w_mem_o


def _sqrelu_mlp(x, w_up, w_down):
    return jnp.square(jax.nn.relu(x @ w_up)) @ w_down


def _layer(x, mem, w_in, q_norm, k_norm, w_attn_branch, w_fourier_branch, w_mix_out,
           ln1_g, ln1_b, w_mem_q, w_mem_k, w_mem_v, w_mem_o, ln2_g, ln2_b,
           w_up, w_down, ln3_g, ln3_b):
    x = _layernorm(ALPHA * x + _token_mixer(x, w_in, q_norm, k_norm, w_attn_branch,
                                            w_fourier_branch, w_mix_out), ln1_g, ln1_b)
    x = _layernorm(ALPHA * x + _memory_xattn(x, mem, w_mem_q, w_mem_k, w_mem_v, w_mem_o), ln2_g, ln2_b)
    x = _layernorm(ALPHA * x + _sqrelu_mlp(x, w_up, w_down), ln3_g, ln3_b)
    return x


def setup_inputs(seed: int = 0) -> dict:
    key = jax.random.key(seed)
    ks = jax.random.split(key, 24)
    f32 = jnp.float32

    def nrm(k, shape, scale):
        return jax.random.normal(k, shape, f32) * scale

    def gain(k):
        return 1.0 + 0.02 * jax.random.normal(k, (DEPTH, D_MODEL), f32)

    def bias(k):
        return 0.02 * jax.random.normal(k, (DEPTH, D_MODEL), f32)

    return {
        "x_prompt": nrm(ks[0], (BATCH, SEQ, D_MODEL), 1.0),
        "x_sample": nrm(ks[1], (DEC_BATCH, DEC_SEQ, D_MODEL), 1.0),
        "mem_prompt": nrm(ks[2], (BATCH, MEM_TOKENS, D_MODEL), 1.0),
        "mem_sample": nrm(ks[3], (DEC_BATCH, MEM_TOKENS, D_MODEL), 1.0),
        "w_in": nrm(ks[4], (DEPTH, D_MODEL, P_IN), D_MODEL ** -0.5),
        "q_norm": 1.0 + 0.02 * jax.random.normal(ks[5], (DEPTH, HEAD_DIM), f32),
        "k_norm": 1.0 + 0.02 * jax.random.normal(ks[6], (DEPTH, HEAD_DIM), f32),
        "w_attn_branch": nrm(ks[7], (DEPTH, ATTN_W, D_MODEL), ATTN_W ** -0.5),
        "w_fourier_branch": nrm(ks[8], (DEPTH, FOURIER_W, D_MODEL), FOURIER_W ** -0.5),
        "w_mix_out": nrm(ks[9], (DEPTH, D_MODEL, D_MODEL), BETA * D_MODEL ** -0.5),
        "ln1_g": gain(ks[10]),
        "ln1_b": bias(ks[11]),
        "w_mem_q": nrm(ks[12], (DEPTH, D_MODEL, MEM_W), D_MODEL ** -0.5),
        "w_mem_k": nrm(ks[13], (DEPTH, D_MODEL, MEM_W), D_MODEL ** -0.5),
        "w_mem_v": nrm(ks[14], (DEPTH, D_MODEL, MEM_W), D_MODEL ** -0.5),
        "w_mem_o": nrm(ks[15], (DEPTH, MEM_W, D_MODEL), BETA * MEM_W ** -0.5),
        "ln2_g": gain(ks[16]),
        "ln2_b": bias(ks[17]),
        "w_up": nrm(ks[18], (DEPTH, D_MODEL, D_FF), D_MODEL ** -0.5),
        "w_down": nrm(ks[19], (DEPTH, D_FF, D_MODEL), BETA * D_FF ** -0.5),
        "ln3_g": gain(ks[20]),
        "ln3_b": bias(ks[21]),
    }


def reference(x_prompt, x_sample, mem_prompt, mem_sample, w_in, q_norm, k_norm,
              w_attn_branch, w_fourier_branch, w_mix_out, ln1_g, ln1_b,
              w_mem_q, w_mem_k, w_mem_v, w_mem_o, ln2_g, ln2_b,
              w_up, w_down, ln3_g, ln3_b):
    y_prompt = x_prompt
    y_sample = x_sample
    for l in range(DEPTH):
        lp = (w_in[l], q_norm[l], k_norm[l], w_attn_branch[l], w_fourier_branch[l], w_mix_out[l],
              ln1_g[l], ln1_b[l], w_mem_q[l], w_mem_k[l], w_mem_v[l], w_mem_o[l],
              ln2_g[l], ln2_b[l], w_up[l], w_down[l], ln3_g[l], ln3_b[l])
        y_prompt = _layer(y_prompt, mem_prompt, *lp)
        y_sample = _layer(y_sample, mem_sample, *lp)
    return (y_prompt, y_sample)
```

```python
import functools
import math

import numpy as np
import jax
import jax.numpy as jnp
from jax import lax
from jax.experimental import pallas as pl
from jax.experimental.pallas import tpu as pltpu

D_MODEL = 1024
GRID_W = 64
N_HEADS = 8
N_KV_HEADS = 2
HEAD_DIM = 64
GROUP = N_HEADS // N_KV_HEADS
ATTN_W = N_HEADS * HEAD_DIM
KV_W = N_KV_HEADS * HEAD_DIM
ROPE_THETA = 10000.0
AXIS_ROT = HEAD_DIM // 2
FOURIER_GROUPS = 4
FOURIER_GROUP_W = 128
FOURIER_W = FOURIER_GROUPS * FOURIER_GROUP_W
QKVU_W = ATTN_W + 2 * KV_W + FOURIER_W
MEM_TOKENS = 256
MEM_HEADS = 4
MEM_HEAD_DIM = D_MODEL // MEM_HEADS
D_FF = 4 * D_MODEL
RMS_EPS = 1e-6
LN_EPS = 1e-5
LOG2E = math.log2(math.e)

V7X_VMEM_BYTES = 64 * 1024 * 1024
VMEM_LIMIT_BYTES = V7X_VMEM_BYTES - 12 * 1024 * 1024
LANES = 128

BF16 = jnp.bfloat16
F32 = jnp.float32


def _dot(a, b):
    return jnp.dot(a, b, preferred_element_type=F32)


def _dot_nt(a, b):
    return lax.dot_general(a, b, (((1,), (1,)), ((), ())), preferred_element_type=F32)


def _resident(shape):
    n = len(shape)
    return pl.BlockSpec(shape, lambda *_: (0,) * n, pipeline_mode=pl.Buffered(1))


def _params(semantics):
    return pltpu.CompilerParams(dimension_semantics=semantics, vmem_limit_bytes=VMEM_LIMIT_BYTES)


def _layernorm(x, g, b):
    mu = jnp.mean(x, axis=-1, keepdims=True)
    xc = x - mu
    var = jnp.mean(xc * xc, axis=-1, keepdims=True)
    return xc * lax.rsqrt(var + LN_EPS) * g + b


def _inproj_kernel(x_ref, w_ref, gain_ref, cos_ref, sin_ref, seg_ref, cdft_ref,
                   qt_ref, k_ref, vt_ref, zc_ref, zs_ref):
    xb = x_ref[...].astype(BF16)
    h = _dot(xb, w_ref[...])
    qk_w = ATTN_W + KV_W
    qk = h[:, :qk_w]
    sq = (qk * qk).astype(BF16)
    seg = seg_ref[...]
    ms = jnp.concatenate(
        [_dot(sq[:, c:c + 2 * LANES], seg) for c in range(0, ATTN_W, 2 * LANES)]
        + [_dot(sq[:, ATTN_W:qk_w], seg[:KV_W, :KV_W])], axis=-1)
    qkn = qk * lax.rsqrt(ms + RMS_EPS) * gain_ref[...]
    cos = cos_ref[...]
    sin = sin_ref[...]
    lane = lax.broadcasted_iota(jnp.int32, cos.shape, 1)
    first_half = (lane % HEAD_DIM) < (HEAD_DIM // 2)
    slabs = []
    for c in range(0, qk_w, LANES):
        xs = qkn[:, c:c + LANES]
        ahead = pltpu.roll(xs, LANES - HEAD_DIM // 2, 1)
        behind = pltpu.roll(xs, HEAD_DIM // 2, 1)
        slabs.append(xs * cos + jnp.where(first_half, ahead, behind) * sin)
    q = jnp.concatenate(slabs[:ATTN_W // LANES], axis=-1) * (HEAD_DIM ** -0.5 * LOG2E)
    qt_ref[...] = q.T.astype(BF16)
    k_ref[...] = slabs[-1].astype(BF16)
    vt_ref[...] = h[:, qk_w:qk_w + KV_W].T.astype(BF16)
    ub = h[:, qk_w + KV_W:].astype(BF16)
    cdft = cdft_ref[...]
    for gi in range(FOURIER_GROUPS):
        c = gi * FOURIER_GROUP_W
        z = _dot(ub[:, c:c + FOURIER_GROUP_W], cdft)
        zc_ref[:, c:c + FOURIER_GROUP_W] = z[:, :FOURIER_GROUP_W].astype(BF16)
        zs_ref[:, c:c + FOURIER_GROUP_W] = z[:, FOURIER_GROUP_W:].astype(BF16)


def _rope_tables(seq_len):
    rows = seq_len // GRID_W
    row = jnp.repeat(jnp.arange(rows, dtype=F32), GRID_W)
    col = jnp.tile(jnp.arange(GRID_W, dtype=F32), rows)
    freqs = ROPE_THETA ** (-jnp.arange(0, AXIS_ROT, 2, dtype=F32) / AXIS_ROT)
    ang = jnp.concatenate([row[:, None] * freqs, col[:, None] * freqs], axis=-1)
    ang = jnp.concatenate([ang, ang], axis=-1)
    cos, sin = jnp.cos(ang), jnp.sin(ang)
    sign = jnp.where(jnp.arange(HEAD_DIM) < HEAD_DIM // 2, -1.0, 1.0).astype(F32)
    reps = LANES // HEAD_DIM
    return jnp.tile(cos, (1, reps)), jnp.tile(sin * sign, (1, reps))


def _segment_mean_matrix():
    idx = np.arange(2 * LANES)
    m = (idx[:, None] // HEAD_DIM == idx[None, :] // HEAD_DIM).astype(np.float32) / HEAD_DIM
    return jnp.asarray(m, dtype=BF16)


def _channel_dft_matrix():
    n = FOURIER_GROUP_W
    jk = np.outer(np.arange(n), np.arange(n)) % n
    ang = 2.0 * np.pi * jk / n
    m = np.concatenate([np.cos(ang), np.sin(ang)], axis=1) / math.sqrt(n)
    return jnp.asarray(m, dtype=BF16)


def _in_projection(x2d, bsz, seq_len, w_qkvu, gain, tm):
    tokens = x2d.shape[0]
    assert seq_len % tm == 0
    tiles_per_seq = seq_len // tm
    cos, sin = _rope_tables(seq_len)
    tok = lambda i: (i, 0)
    pos = lambda i: (i % tiles_per_seq, 0)
    tr = lambda i: (i // tiles_per_seq, 0, i % tiles_per_seq)
    return pl.pallas_call(
        _inproj_kernel,
        grid=(tokens // tm,),
        in_specs=[
            pl.BlockSpec((tm, D_MODEL), tok),
            _resident((D_MODEL, QKVU_W)),
            _resident((1, ATTN_W + KV_W)),
            pl.BlockSpec((tm, LANES), pos),
            pl.BlockSpec((tm, LANES), pos),
            _resident((2 * LANES, 2 * LANES)),
            _resident((FOURIER_GROUP_W, 2 * FOURIER_GROUP_W)),
        ],
        out_specs=[
            pl.BlockSpec((None, ATTN_W, tm), tr),
            pl.BlockSpec((tm, KV_W), tok),
            pl.BlockSpec((None, KV_W, tm), tr),
            pl.BlockSpec((tm, FOURIER_W), tok),
            pl.BlockSpec((tm, FOURIER_W), tok),
        ],
        out_shape=[
            jax.ShapeDtypeStruct((bsz, ATTN_W, seq_len), BF16),
            jax.ShapeDtypeStruct((tokens, KV_W), BF16),
            jax.ShapeDtypeStruct((bsz, KV_W, seq_len), BF16),
            jax.ShapeDtypeStruct((tokens, FOURIER_W), BF16),
            jax.ShapeDtypeStruct((tokens, FOURIER_W), BF16),
        ],
        compiler_params=_params(("parallel",)),
        name="in_projection",
    )(x2d, w_qkvu, gain, cos, sin, _segment_mean_matrix(), _channel_dft_matrix())


SUM_ROWS = 16


def _attention_kernel(qt_ref, k_ref, vt_ref, o_ref, acc_ref, *, tk):
    tq = qt_ref.shape[-1]
    seq_len = k_ref.shape[0]
    kv_head = pl.program_id(1)
    row_head = lax.broadcasted_iota(jnp.int32, (KV_W, tq), 0) // HEAD_DIM
    q_pad = []
    for h in range(GROUP):
        qh = qt_ref[h * HEAD_DIM:(h + 1) * HEAD_DIM, :]
        q_rep = jnp.concatenate([qh] * N_KV_HEADS, axis=0)
        q_pad.append(jnp.where(row_head == kv_head, q_rep, jnp.zeros_like(q_rep)))
    acc_ref[...] = jnp.zeros_like(acc_ref)
    ones = jnp.ones((SUM_ROWS, tk), BF16)

    def chunk(c, m_prev):
        start = pl.multiple_of(c * tk, tk)
        kc = k_ref[pl.ds(start, tk), :]
        vc = jnp.concatenate([vt_ref[:, pl.ds(start, tk)], ones], axis=0)
        m_next = []
        for h in range(GROUP):
            s = _dot(kc, q_pad[h])
            m_new = jnp.maximum(m_prev[h], jnp.max(s, axis=0, keepdims=True))
            alpha = jnp.exp2(m_prev[h] - m_new)
            p = jnp.exp2(s - m_new).astype(BF16)
            acc_ref[h] = alpha * acc_ref[h] + _dot(vc, p)
            m_next.append(m_new)
        return tuple(m_next)

    m0 = tuple(jnp.full((1, tq), -jnp.inf, F32) for _ in range(GROUP))
    lax.fori_loop(0, seq_len // tk, chunk, m0)
    outs = []
    for h in range(GROUP):
        a = acc_ref[h]
        outs.append(a[:HEAD_DIM] / a[HEAD_DIM:HEAD_DIM + 1])
    o_ref[...] = jnp.concatenate(outs, axis=0).T.astype(BF16)


def _attention(qt, k, vt, tq, tk):
    bsz, _, seq_len = qt.shape
    gw = GROUP * HEAD_DIM
    return pl.pallas_call(
        functools.partial(_attention_kernel, tk=tk),
        grid=(bsz, N_KV_HEADS, seq_len // tq),
        in_specs=[
            pl.BlockSpec((None, gw, tq), lambda b, g, i: (b, g, i)),
            pl.BlockSpec((None, seq_len, KV_W), lambda b, g, i: (b, 0, 0)),
            pl.BlockSpec((None, HEAD_DIM, seq_len), lambda b, g, i: (b, g, 0)),
        ],
        out_specs=pl.BlockSpec((None, tq, gw), lambda b, g, i: (b, i, g)),
        out_shape=jax.ShapeDtypeStruct((bsz, seq_len, ATTN_W), BF16),
        scratch_shapes=[pltpu.VMEM((GROUP, HEAD_DIM + SUM_ROWS, tq), F32)],
        compiler_params=_params(("parallel", "parallel", "parallel")),
        name="gqa_attention",
    )(qt, k, vt)


def _dft_stage1_kernel(zc_ref, zs_ref, m1_ref, twc_ref, tws_ref, yr_ref, yi_ref):
    n1 = zc_ref.shape[0]
    x = jnp.concatenate([zc_ref[...], zs_ref[...]], axis=0)
    y = _dot(m1_ref[...], x)
    yr, yi = y[:n1], y[n1:]
    c, s = twc_ref[...], tws_ref[...]
    yr_ref[...] = (yr * c + yi * s).astype(BF16)
    yi_ref[...] = (yi * c - yr * s).astype(BF16)


def _dft_stage2_kernel(yr_ref, yi_ref, m2_ref, f_ref):
    tk1 = yr_ref.shape[0]
    m2 = m2_ref[...]
    for j in range(tk1):
        y = jnp.concatenate([yr_ref[j], yi_ref[j]], axis=0)
        f_ref[:, j * FOURIER_W:(j + 1) * FOURIER_W] = _dot(m2, y).astype(BF16)


def _dft_factors(seq_len):
    n1 = 1 << (int(math.log2(seq_len)) // 2)
    n2 = seq_len // n1
    assert n1 * n2 == seq_len
    return n1, n2


def _dft_matrices(n1, n2):
    a1 = 2.0 * np.pi * (np.outer(np.arange(n1), np.arange(n1)) % n1) / n1
    c1, s1 = np.cos(a1) / math.sqrt(n1), np.sin(a1) / math.sqrt(n1)
    m1 = np.block([[c1, -s1], [-s1, -c1]])
    a2 = 2.0 * np.pi * (np.outer(np.arange(n2), np.arange(n2)) % n2) / n2
    m2 = np.concatenate([np.cos(a2), np.sin(a2)], axis=1) / math.sqrt(n2)
    return jnp.asarray(m1, dtype=BF16), jnp.asarray(m2, dtype=BF16)


def _twiddles(n1, n2):
    seq_len = n1 * n2
    k1 = jnp.arange(n1, dtype=jnp.int32)[:, None]
    nn = jnp.arange(n2, dtype=jnp.int32)[None, :]
    ang = ((k1 * nn) % seq_len).astype(F32) * (2.0 * math.pi / seq_len)
    rep = lambda t: jnp.repeat(t, FOURIER_W, axis=1)
    return rep(jnp.cos(ang)), rep(jnp.sin(ang))


def _sequence_dft(zc, zs, bsz, seq_len, tn2, tk1):
    n1, n2 = _dft_factors(seq_len)
    m1, m2 = _dft_matrices(n1, n2)
    twc, tws = _twiddles(n1, n2)
    width = n2 * FOURIER_W
    blk = tn2 * FOURIER_W
    data = pl.BlockSpec((None, n1, blk), lambda j, b: (b, 0, j))
    tw = pl.BlockSpec((n1, blk), lambda j, b: (0, j))
    yr, yi = pl.pallas_call(
        _dft_stage1_kernel,
        grid=(n2 // tn2, bsz),
        in_specs=[data, data, _resident((2 * n1, 2 * n1)), tw, tw],
        out_specs=[data, data],
        out_shape=[jax.ShapeDtypeStruct((bsz, n1, width), BF16)] * 2,
        compiler_params=_params(("parallel", "parallel")),
        name="dft_stage1",
    )(zc.reshape(bsz, n1, width), zs.reshape(bsz, n1, width), m1, twc, tws)
    planes = pl.BlockSpec((None, tk1, n2, FOURIER_W), lambda b, j: (b, j, 0, 0))
    f = pl.pallas_call(
        _dft_stage2_kernel,
        grid=(bsz, n1 // tk1),
        in_specs=[planes, planes, _resident((n2, 2 * n2))],
        out_specs=pl.BlockSpec((None, n2, tk1 * FOURIER_W), lambda b, j: (b, 0, j)),
        out_shape=jax.ShapeDtypeStruct((bsz, n2, n1 * FOURIER_W), BF16),
        compiler_params=_params(("parallel", "parallel")),
        name="dft_stage2",
    )(yr.reshape(bsz, n1, n2, FOURIER_W), yi.reshape(bsz, n1, n2, FOURIER_W), m2)
    return f.reshape(bsz * seq_len, FOURIER_W)


def _mem_kv_kernel(mem_ref, w_ref, k_ref, v_ref):
    y = _dot(mem_ref[...].astype(BF16), w_ref[...])
    k_ref[...] = y[:, :D_MODEL].astype(BF16)
    v_ref[...] = y[:, D_MODEL:].astype(BF16)


def _mem_kv(mem2d, w_kv):
    rows = mem2d.shape[0]
    blk = pl.BlockSpec((MEM_TOKENS, D_MODEL), lambda i: (i, 0))
    return pl.pallas_call(
        _mem_kv_kernel,
        grid=(rows // MEM_TOKENS,),
        in_specs=[blk, _resident((D_MODEL, 2 * D_MODEL))],
        out_specs=[blk, blk],
        out_shape=[jax.ShapeDtypeStruct((rows, D_MODEL), BF16)] * 2,
        compiler_params=_params(("parallel",)),
        name="mem_kv",
    )(mem2d, w_kv)


def _mix_xattn_kernel(x_ref, o_ref, f_ref, wg_ref, wab_ref, wfb_ref, wmix_ref, ln1g_ref, ln1b_ref,
                      wq_ref, km_ref, vm_ref, wo_ref, ln2g_ref, ln2b_ref, y_ref, *, alpha):
    x = x_ref[...]
    gates = _dot(x.astype(BF16), wg_ref[...])
    y_attn = _dot(o_ref[...], wab_ref[...])
    y_four = _dot(f_ref[...], wfb_ref[...])
    merged = (jax.nn.sigmoid(gates[:, :D_MODEL]) * y_attn
              + jax.nn.sigmoid(gates[:, D_MODEL:]) * y_four)
    x1 = _layernorm(alpha * x + _dot(merged.astype(BF16), wmix_ref[...]), ln1g_ref[...], ln1b_ref[...])
    q = (_dot(x1.astype(BF16), wq_ref[...]) * (MEM_HEAD_DIM ** -0.5 * LOG2E)).astype(BF16)
    heads = []
    for h in range(MEM_HEADS):
        c = h * MEM_HEAD_DIM
        s = _dot_nt(q[:, c:c + MEM_HEAD_DIM], km_ref[:, c:c + MEM_HEAD_DIM])
        p = jnp.exp2(s - jnp.max(s, axis=-1, keepdims=True))
        l = jnp.sum(p, axis=-1, keepdims=True)
        heads.append(_dot(p.astype(BF16), vm_ref[:, c:c + MEM_HEAD_DIM]) / l)
    ctx = jnp.concatenate(heads, axis=-1).astype(BF16)
    y_ref[...] = _layernorm(alpha * x1 + _dot(ctx, wo_ref[...]), ln2g_ref[...], ln2b_ref[...])


def _mix_xattn(x2d, o2d, f2d, seq_len, wg, wab, wfb, wmix, ln1g, ln1b, wq, km, vm, wo, ln2g, ln2b, alpha, tm):
    tokens = x2d.shape[0]
    assert seq_len % tm == 0
    tiles_per_seq = seq_len // tm
    tok = lambda w: pl.BlockSpec((tm, w), lambda i: (i, 0))
    mem = pl.BlockSpec((MEM_TOKENS, D_MODEL), lambda i: (i // tiles_per_seq, 0))
    vec = _resident((1, D_MODEL))
    return pl.pallas_call(
        functools.partial(_mix_xattn_kernel, alpha=alpha),
        grid=(tokens // tm,),
        in_specs=[tok(D_MODEL), tok(ATTN_W), tok(FOURIER_W),
                  _resident((D_MODEL, 2 * D_MODEL)), _resident((ATTN_W, D_MODEL)),
                  _resident((FOURIER_W, D_MODEL)), _resident((D_MODEL, D_MODEL)), vec, vec,
                  _resident((D_MODEL, D_MODEL)), mem, mem, _resident((D_MODEL, D_MODEL)), vec, vec],
        out_specs=tok(D_MODEL),
        out_shape=jax.ShapeDtypeStruct((tokens, D_MODEL), F32),
        compiler_params=_params(("parallel",)),
        name="mix_xattn",
    )(x2d, o2d, f2d, wg, wab, wfb, wmix, ln1g, ln1b, wq, km, vm, wo, ln2g, ln2b)


FF_CHUNK = 1024


def _mlp_kernel(x_ref, wup_ref, wdown_ref, g_ref, b_ref, y_ref, *, alpha):
    x = x_ref[...]
    xb = x.astype(BF16)
    y = alpha * x
    for c in range(0, D_FF, FF_CHUNK):
        h = jnp.maximum(_dot(xb, wup_ref[:, c:c + FF_CHUNK]), 0.0)
        y = y + _dot((h * h).astype(BF16), wdown_ref[c:c + FF_CHUNK, :])
    y_ref[...] = _layernorm(y, g_ref[...], b_ref[...])


def _mlp(x2d, wup, wdown, g, b, alpha, tm):
    tokens = x2d.shape[0]
    tok = pl.BlockSpec((tm, D_MODEL), lambda i: (i, 0))
    vec = _resident((1, D_MODEL))
    return pl.pallas_call(
        functools.partial(_mlp_kernel, alpha=alpha),
        grid=(tokens // tm,),
        in_specs=[tok, _resident((D_MODEL, D_FF)), _resident((D_FF, D_MODEL)), vec, vec],
        out_specs=tok,
        out_shape=jax.ShapeDtypeStruct((tokens, D_MODEL), F32),
        compiler_params=_params(("parallel",)),
        name="mlp",
    )(x2d, wup, wdown, g, b)


def _tile(n, want):
    t = min(n, want)
    assert n % t == 0
    return t


def _layer(x, mem, w, alpha):
    bsz, seq_len, _ = x.shape
    x2d = x.reshape(bsz * seq_len, D_MODEL)
    tm = _tile(seq_len, 512)
    qt, k, vt, zc, zs = _in_projection(x2d, bsz, seq_len, w["qkvu"], w["qk_gain"], tm)
    o = _attention(qt, k.reshape(bsz, seq_len, KV_W), vt, _tile(seq_len, 256), _tile(seq_len, 512))
    n1, _ = _dft_factors(seq_len)
    f = _sequence_dft(zc, zs, bsz, seq_len, _tile(n1, 8), _tile(n1, 8))
    km, vm = _mem_kv(mem.reshape(bsz * MEM_TOKENS, D_MODEL), w["mem_kv"])
    x2 = _mix_xattn(x2d, o.reshape(bsz * seq_len, ATTN_W), f, seq_len, w["gate"], w["attn_branch"],
                    w["fourier_branch"], w["mix_out"], w["ln1_g"], w["ln1_b"], w["mem_q"], km, vm,
                    w["mem_o"], w["ln2_g"], w["ln2_b"], alpha, tm)
    y = _mlp(x2, w["up"], w["down"], w["ln3_g"], w["ln3_b"], alpha, tm)
    return y.reshape(bsz, seq_len, D_MODEL)


def kernel(x_prompt, x_sample, mem_prompt, mem_sample, w_in, q_norm, k_norm, w_attn_branch, w_fourier_branch, w_mix_out, ln1_g, ln1_b, w_mem_q, w_mem_k, w_mem_v, w_mem_o, ln2_g, ln2_b, w_up, w_down, ln3_g, ln3_b):
    depth = w_in.shape[0]
    alpha = float((2 * depth) ** 0.25)
    y_prompt, y_sample = x_prompt, x_sample
    for l in range(depth):
        row = lambda v: v[l].reshape(1, -1).astype(F32)
        w = {
            "qkvu": w_in[l, :, :QKVU_W].astype(BF16),
            "gate": w_in[l, :, QKVU_W:].astype(BF16),
            "qk_gain": jnp.concatenate([jnp.tile(q_norm[l], N_HEADS), jnp.tile(k_norm[l], N_KV_HEADS)]).reshape(1, -1).astype(F32),
            "attn_branch": w_attn_branch[l].astype(BF16),
            "fourier_branch": w_fourier_branch[l].astype(BF16),
            "mix_out": w_mix_out[l].astype(BF16),
            "ln1_g": row(ln1_g), "ln1_b": row(ln1_b),
            "mem_q": w_mem_q[l].astype(BF16),
            "mem_kv": jnp.concatenate([w_mem_k[l], w_mem_v[l]], axis=1).astype(BF16),
            "mem_o": w_mem_o[l].astype(BF16),
            "ln2_g": row(ln2_g), "ln2_b": row(ln2_b),
            "up": w_up[l].astype(BF16), "down": w_down[l].astype(BF16),
            "ln3_g": row(ln3_g), "ln3_b": row(ln3_b),
        }
        y_prompt = _layer(y_prompt, mem_prompt, w, alpha)
        y_sample = _layer(y_sample, mem_sample, w, alpha)
    return (y_prompt, y_sample)
```

```python
import functools
import math

import numpy as np
import jax
import jax.numpy as jnp
from jax import lax
from jax.experimental import pallas as pl
from jax.experimental.pallas import tpu as pltpu

D_MODEL = 1024
GRID_W = 64
N_HEADS = 8
N_KV_HEADS = 2
HEAD_DIM = 64
GROUP = N_HEADS // N_KV_HEADS
ATTN_W = N_HEADS * HEAD_DIM
KV_W = N_KV_HEADS * HEAD_DIM
ROPE_THETA = 10000.0
AXIS_ROT = HEAD_DIM // 2
FOURIER_GROUPS = 4
FOURIER_GROUP_W = 128
FOURIER_W = FOURIER_GROUPS * FOURIER_GROUP_W
QKVU_W = ATTN_W + 2 * KV_W + FOURIER_W
MEM_TOKENS = 256
MEM_HEADS = 4
MEM_HEAD_DIM = D_MODEL // MEM_HEADS
D_FF = 4 * D_MODEL
RMS_EPS = 1e-6
LN_EPS = 1e-5
LOG2E = math.log2(math.e)

V7X_VMEM_BYTES = 64 * 1024 * 1024
VMEM_LIMIT_BYTES = V7X_VMEM_BYTES - 12 * 1024 * 1024
LANES = 128

BF16 = jnp.bfloat16
F32 = jnp.float32


def _dot(a, b):
    return jnp.dot(a, b, preferred_element_type=F32)


def _dot_nt(a, b):
    return lax.dot_general(a, b, (((1,), (1,)), ((), ())), preferred_element_type=F32)


def _resident(shape):
    n = len(shape)
    return pl.BlockSpec(shape, lambda *_: (0,) * n, pipeline_mode=pl.Buffered(1))


def _params(semantics):
    return pltpu.CompilerParams(dimension_semantics=semantics, vmem_limit_bytes=VMEM_LIMIT_BYTES)


def _layernorm(x, g, b):
    mu = jnp.mean(x, axis=-1, keepdims=True)
    xc = x - mu
    var = jnp.mean(xc * xc, axis=-1, keepdims=True)
    return xc * lax.rsqrt(var + LN_EPS) * g + b


def _inproj_kernel(x_ref, w_ref, gain_ref, cos_ref, sin_ref, seg_ref, cdft_ref,
                   qt_ref, k_ref, vt_ref, zc_ref, zs_ref):
    xb = x_ref[...].astype(BF16)
    h = _dot(xb, w_ref[...])
    qk_w = ATTN_W + KV_W
    qk = h[:, :qk_w]
    sq = (qk * qk).astype(BF16)
    seg = seg_ref[...]
    ms = jnp.concatenate(
        [_dot(sq[:, c:c + 2 * LANES], seg) for c in range(0, ATTN_W, 2 * LANES)]
        + [_dot(sq[:, ATTN_W:qk_w], seg[:KV_W, :KV_W])], axis=-1)
    qkn = qk * lax.rsqrt(ms + RMS_EPS) * gain_ref[...]
    cos = cos_ref[...]
    sin = sin_ref[...]
    lane = lax.broadcasted_iota(jnp.int32, cos.shape, 1)
    first_half = (lane % HEAD_DIM) < (HEAD_DIM // 2)
    slabs = []
    for c in range(0, qk_w, LANES):
        xs = qkn[:, c:c + LANES]
        ahead = pltpu.roll(xs, LANES - HEAD_DIM // 2, 1)
        behind = pltpu.roll(xs, HEAD_DIM // 2, 1)
        slabs.append(xs * cos + jnp.where(first_half, ahead, behind) * sin)
    q = jnp.concatenate(slabs[:ATTN_W // LANES], axis=-1) * (HEAD_DIM ** -0.5 * LOG2E)
    qt_ref[...] = q.T.astype(BF16)
    k_ref[...] = slabs[-1].astype(BF16)
    vt_ref[...] = h[:, qk_w:qk_w + KV_W].T.astype(BF16)
    ub = h[:, qk_w + KV_W:].astype(BF16)
    cdft = cdft_ref[...]
    for gi in range(FOURIER_GROUPS):
        c = gi * FOURIER_GROUP_W
        z = _dot(ub[:, c:c + FOURIER_GROUP_W], cdft)
        zc_ref[:, c:c + FOURIER_GROUP_W] = z[:, :FOURIER_GROUP_W].astype(BF16)
        zs_ref[:, c:c + FOURIER_GROUP_W] = z[:, FOURIER_GROUP_W:].astype(BF16)


def _rope_tables(seq_len):
    rows = seq_len // GRID_W
    row = jnp.repeat(jnp.arange(rows, dtype=F32), GRID_W)
    col = jnp.tile(jnp.arange(GRID_W, dtype=F32), rows)
    freqs = ROPE_THETA ** (-jnp.arange(0, AXIS_ROT, 2, dtype=F32) / AXIS_ROT)
    ang = jnp.concatenate([row[:, None] * freqs, col[:, None] * freqs], axis=-1)
    ang = jnp.concatenate([ang, ang], axis=-1)
    cos, sin = jnp.cos(ang), jnp.sin(ang)
    sign = jnp.where(jnp.arange(HEAD_DIM) < HEAD_DIM // 2, -1.0, 1.0).astype(F32)
    reps = LANES // HEAD_DIM
    return jnp.tile(cos, (1, reps)), jnp.tile(sin * sign, (1, reps))


def _segment_mean_matrix():
    idx = np.arange(2 * LANES)
    m = (idx[:, None] // HEAD_DIM == idx[None, :] // HEAD_DIM).astype(np.float32) / HEAD_DIM
    return jnp.asarray(m, dtype=BF16)


def _channel_dft_matrix():
    n = FOURIER_GROUP_W
    jk = np.outer(np.arange(n), np.arange(n)) % n
    ang = 2.0 * np.pi * jk / n
    m = np.concatenate([np.cos(ang), np.sin(ang)], axis=1) / math.sqrt(n)
    return jnp.asarray(m, dtype=BF16)


def _in_projection(x2d, bsz, seq_len, w_qkvu, gain, tm):
    tokens = x2d.shape[0]
    assert seq_len % tm == 0
    tiles_per_seq = seq_len // tm
    cos, sin = _rope_tables(seq_len)
    tok = lambda i: (i, 0)
    pos = lambda i: (i % tiles_per_seq, 0)
    tr = lambda i: (i // tiles_per_seq, 0, i % tiles_per_seq)
    return pl.pallas_call(
        _inproj_kernel,
        grid=(tokens // tm,),
        in_specs=[
            pl.BlockSpec((tm, D_MODEL), tok),
            _resident((D_MODEL, QKVU_W)),
            _resident((1, ATTN_W + KV_W)),
            pl.BlockSpec((tm, LANES), pos),
            pl.BlockSpec((tm, LANES), pos),
            _resident((2 * LANES, 2 * LANES)),
            _resident((FOURIER_GROUP_W, 2 * FOURIER_GROUP_W)),
        ],
        out_specs=[
            pl.BlockSpec((None, ATTN_W, tm), tr),
            pl.BlockSpec((tm, KV_W), tok),
            pl.BlockSpec((None, KV_W, tm), tr),
            pl.BlockSpec((tm, FOURIER_W), tok),
            pl.BlockSpec((tm, FOURIER_W), tok),
        ],
        out_shape=[
            jax.ShapeDtypeStruct((bsz, ATTN_W, seq_len), BF16),
            jax.ShapeDtypeStruct((tokens, KV_W), BF16),
            jax.ShapeDtypeStruct((bsz, KV_W, seq_len), BF16),
            jax.ShapeDtypeStruct((tokens, FOURIER_W), BF16),
            jax.ShapeDtypeStruct((tokens, FOURIER_W), BF16),
        ],
        compiler_params=_params(("parallel",)),
        name="in_projection",
    )(x2d, w_qkvu, gain, cos, sin, _segment_mean_matrix(), _channel_dft_matrix())


SUM_ROWS = 16


def _attention_kernel(qt_ref, k_ref, vt_ref, o_ref, qpad_ref, sa_ref, sb_ref, acc_ref, *, tk):
    tq = qt_ref.shape[-1]
    seq_len = k_ref.shape[0]
    n_chunks = seq_len // tk
    assert n_chunks % 2 == 0
    kv_head = pl.program_id(1)
    row_head = lax.broadcasted_iota(jnp.int32, (KV_W, tq), 0) // HEAD_DIM
    for h in range(GROUP):
        qh = qt_ref[h * HEAD_DIM:(h + 1) * HEAD_DIM, :]
        q_rep = jnp.concatenate([qh] * N_KV_HEADS, axis=0)
        qpad_ref[h] = jnp.where(row_head == kv_head, q_rep, jnp.zeros_like(q_rep))
    acc_ref[...] = jnp.zeros_like(acc_ref)
    ones = jnp.ones((SUM_ROWS, tk), BF16)

    def scores(c, s_ref):
        kc = k_ref[pl.ds(pl.multiple_of(c * tk, tk), tk), :]
        maxima = []
        for h in range(GROUP):
            s = _dot(kc, qpad_ref[h])
            s_ref[h] = s
            maxima.append(jnp.max(s, axis=0, keepdims=True))
        return tuple(maxima)

    def accumulate(c, s_ref, m_prev, m_chunk):
        vc = jnp.concatenate([vt_ref[:, pl.ds(pl.multiple_of(c * tk, tk), tk)], ones], axis=0)
        m_next = []
        for h in range(GROUP):
            m_new = jnp.maximum(m_prev[h], m_chunk[h])
            alpha = jnp.exp2(m_prev[h] - m_new)
            p = jnp.exp2(s_ref[h] - m_new).astype(BF16)
            acc_ref[h] = alpha * acc_ref[h] + _dot(vc, p)
            m_next.append(m_new)
        return tuple(m_next)

    def pair(j, carry):
        m_run, max_a = carry
        c = 2 * j
        max_b = scores(c + 1, sb_ref)
        m_run = accumulate(c, sa_ref, m_run, max_a)
        max_a = scores(c + 2, sa_ref)
        m_run = accumulate(c + 1, sb_ref, m_run, max_b)
        return m_run, max_a

    m0 = tuple(jnp.full((1, tq), -jnp.inf, F32) for _ in range(GROUP))
    m_run, max_a = lax.fori_loop(0, n_chunks // 2 - 1, pair, (m0, scores(0, sa_ref)))
    max_b = scores(n_chunks - 1, sb_ref)
    m_run = accumulate(n_chunks - 2, sa_ref, m_run, max_a)
    accumulate(n_chunks - 1, sb_ref, m_run, max_b)
    outs = []
    for h in range(GROUP):
        a = acc_ref[h]
        outs.append(a[:HEAD_DIM] / a[HEAD_DIM:HEAD_DIM + 1])
    o_ref[...] = jnp.concatenate(outs, axis=0).T.astype(BF16)


def _attention(qt, k, vt, tq, tk):
    bsz, _, seq_len = qt.shape
    gw = GROUP * HEAD_DIM
    return pl.pallas_call(
        functools.partial(_attention_kernel, tk=tk),
        grid=(bsz, N_KV_HEADS, seq_len // tq),
        in_specs=[
            pl.BlockSpec((None, gw, tq), lambda b, g, i: (b, g, i)),
            pl.BlockSpec((None, seq_len, KV_W), lambda b, g, i: (b, 0, 0)),
            pl.BlockSpec((None, HEAD_DIM, seq_len), lambda b, g, i: (b, g, 0)),
        ],
        out_specs=pl.BlockSpec((None, tq, gw), lambda b, g, i: (b, i, g)),
        out_shape=jax.ShapeDtypeStruct((bsz, seq_len, ATTN_W), BF16),
        scratch_shapes=[pltpu.VMEM((GROUP, KV_W, tq), BF16),
                        pltpu.VMEM((GROUP, tk, tq), F32),
                        pltpu.VMEM((GROUP, tk, tq), F32),
                        pltpu.VMEM((GROUP, HEAD_DIM + SUM_ROWS, tq), F32)],
        compiler_params=_params(("parallel", "parallel", "parallel")),
        name="gqa_attention",
    )(qt, k, vt)


def _dft_stage1_kernel(zc_ref, zs_ref, m1_ref, twc_ref, tws_ref, yr_ref, yi_ref):
    n1 = zc_ref.shape[0]
    x = jnp.concatenate([zc_ref[...], zs_ref[...]], axis=0)
    y = _dot(m1_ref[...], x)
    yr, yi = y[:n1], y[n1:]
    c, s = twc_ref[...], tws_ref[...]
    yr_ref[...] = (yr * c + yi * s).astype(BF16)
    yi_ref[...] = (yi * c - yr * s).astype(BF16)


def _dft_stage2_kernel(yr_ref, yi_ref, m2_ref, f_ref):
    tk1 = yr_ref.shape[0]
    m2 = m2_ref[...]
    for j in range(tk1):
        y = jnp.concatenate([yr_ref[j], yi_ref[j]], axis=0)
        f_ref[:, j * FOURIER_W:(j + 1) * FOURIER_W] = _dot(m2, y).astype(BF16)


def _dft_factors(seq_len):
    n1 = 1 << (int(math.log2(seq_len)) // 2)
    n2 = seq_len // n1
    assert n1 * n2 == seq_len
    return n1, n2


def _dft_matrices(n1, n2):
    a1 = 2.0 * np.pi * (np.outer(np.arange(n1), np.arange(n1)) % n1) / n1
    c1, s1 = np.cos(a1) / math.sqrt(n1), np.sin(a1) / math.sqrt(n1)
    m1 = np.block([[c1, -s1], [-s1, -c1]])
    a2 = 2.0 * np.pi * (np.outer(np.arange(n2), np.arange(n2)) % n2) / n2
    m2 = np.concatenate([np.cos(a2), np.sin(a2)], axis=1) / math.sqrt(n2)
    return jnp.asarray(m1, dtype=BF16), jnp.asarray(m2, dtype=BF16)


def _twiddles(n1, n2):
    seq_len = n1 * n2
    k1 = jnp.arange(n1, dtype=jnp.int32)[:, None]
    nn = jnp.arange(n2, dtype=jnp.int32)[None, :]
    ang = ((k1 * nn) % seq_len).astype(F32) * (2.0 * math.pi / seq_len)
    rep = lambda t: jnp.repeat(t, FOURIER_W, axis=1)
    return rep(jnp.cos(ang)), rep(jnp.sin(ang))


def _sequence_dft(zc, zs, bsz, seq_len, tn2, tk1):
    n1, n2 = _dft_factors(seq_len)
    m1, m2 = _dft_matrices(n1, n2)
    twc, tws = _twiddles(n1, n2)
    width = n2 * FOURIER_W
    blk = tn2 * FOURIER_W
    data = pl.BlockSpec((None, n1, blk), lambda j, b: (b, 0, j))
    tw = pl.BlockSpec((n1, blk), lambda j, b: (0, j))
    yr, yi = pl.pallas_call(
        _dft_stage1_kernel,
        grid=(n2 // tn2, bsz),
        in_specs=[data, data, _resident((2 * n1, 2 * n1)), tw, tw],
        out_specs=[data, data],
        out_shape=[jax.ShapeDtypeStruct((bsz, n1, width), BF16)] * 2,
        compiler_params=_params(("parallel", "parallel")),
        name="dft_stage1",
    )(zc.reshape(bsz, n1, width), zs.reshape(bsz, n1, width), m1, twc, tws)
    planes = pl.BlockSpec((None, tk1, n2, FOURIER_W), lambda b, j: (b, j, 0, 0))
    f = pl.pallas_call(
        _dft_stage2_kernel,
        grid=(bsz, n1 // tk1),
        in_specs=[planes, planes, _resident((n2, 2 * n2))],
        out_specs=pl.BlockSpec((None, n2, tk1 * FOURIER_W), lambda b, j: (b, 0, j)),
        out_shape=jax.ShapeDtypeStruct((bsz, n2, n1 * FOURIER_W), BF16),
        compiler_params=_params(("parallel", "parallel")),
        name="dft_stage2",
    )(yr.reshape(bsz, n1, n2, FOURIER_W), yi.reshape(bsz, n1, n2, FOURIER_W), m2)
    return f.reshape(bsz * seq_len, FOURIER_W)


def _mem_kv_kernel(mem_ref, w_ref, k_ref, v_ref):
    y = _dot(mem_ref[...].astype(BF16), w_ref[...])
    k_ref[...] = y[:, :D_MODEL].astype(BF16)
    v_ref[...] = y[:, D_MODEL:].astype(BF16)


def _mem_kv(mem2d, w_kv):
    rows = mem2d.shape[0]
    blk = pl.BlockSpec((MEM_TOKENS, D_MODEL), lambda i: (i, 0))
    return pl.pallas_call(
        _mem_kv_kernel,
        grid=(rows // MEM_TOKENS,),
        in_specs=[blk, _resident((D_MODEL, 2 * D_MODEL))],
        out_specs=[blk, blk],
        out_shape=[jax.ShapeDtypeStruct((rows, D_MODEL), BF16)] * 2,
        compiler_params=_params(("parallel",)),
        name="mem_kv",
    )(mem2d, w_kv)


def _mix_xattn_kernel(x_ref, o_ref, f_ref, wg_ref, wab_ref, wfb_ref, wmix_ref, ln1g_ref, ln1b_ref,
                      wq_ref, km_ref, vm_ref, wo_ref, ln2g_ref, ln2b_ref, y_ref, *, alpha):
    x = x_ref[...]
    gates = _dot(x.astype(BF16), wg_ref[...])
    y_attn = _dot(o_ref[...], wab_ref[...])
    y_four = _dot(f_ref[...], wfb_ref[...])
    merged = (jax.nn.sigmoid(gates[:, :D_MODEL]) * y_attn
              + jax.nn.sigmoid(gates[:, D_MODEL:]) * y_four)
    x1 = _layernorm(alpha * x + _dot(merged.astype(BF16), wmix_ref[...]), ln1g_ref[...], ln1b_ref[...])
    q = (_dot(x1.astype(BF16), wq_ref[...]) * (MEM_HEAD_DIM ** -0.5 * LOG2E)).astype(BF16)
    heads = []
    for h in range(MEM_HEADS):
        c = h * MEM_HEAD_DIM
        s = _dot_nt(q[:, c:c + MEM_HEAD_DIM], km_ref[:, c:c + MEM_HEAD_DIM])
        p = jnp.exp2(s - jnp.max(s, axis=-1, keepdims=True))
        l = jnp.sum(p, axis=-1, keepdims=True)
        heads.append(_dot(p.astype(BF16), vm_ref[:, c:c + MEM_HEAD_DIM]) / l)
    ctx = jnp.concatenate(heads, axis=-1).astype(BF16)
    y_ref[...] = _layernorm(alpha * x1 + _dot(ctx, wo_ref[...]), ln2g_ref[...], ln2b_ref[...])


def _mix_xattn(x2d, o2d, f2d, seq_len, wg, wab, wfb, wmix, ln1g, ln1b, wq, km, vm, wo, ln2g, ln2b, alpha, tm):
    tokens = x2d.shape[0]
    assert seq_len % tm == 0
    tiles_per_seq = seq_len // tm
    tok = lambda w: pl.BlockSpec((tm, w), lambda i: (i, 0))
    mem = pl.BlockSpec((MEM_TOKENS, D_MODEL), lambda i: (i // tiles_per_seq, 0))
    vec = _resident((1, D_MODEL))
    return pl.pallas_call(
        functools.partial(_mix_xattn_kernel, alpha=alpha),
        grid=(tokens // tm,),
        in_specs=[tok(D_MODEL), tok(ATTN_W), tok(FOURIER_W),
                  _resident((D_MODEL, 2 * D_MODEL)), _resident((ATTN_W, D_MODEL)),
                  _resident((FOURIER_W, D_MODEL)), _resident((D_MODEL, D_MODEL)), vec, vec,
                  _resident((D_MODEL, D_MODEL)), mem, mem, _resident((D_MODEL, D_MODEL)), vec, vec],
        out_specs=tok(D_MODEL),
        out_shape=jax.ShapeDtypeStruct((tokens, D_MODEL), F32),
        compiler_params=_params(("parallel",)),
        name="mix_xattn",
    )(x2d, o2d, f2d, wg, wab, wfb, wmix, ln1g, ln1b, wq, km, vm, wo, ln2g, ln2b)


FF_CHUNK = 1024


def _mlp_kernel(x_ref, wup_ref, wdown_ref, g_ref, b_ref, y_ref, *, alpha):
    x = x_ref[...]
    xb = x.astype(BF16)
    y = alpha * x
    for c in range(0, D_FF, FF_CHUNK):
        h = jnp.maximum(_dot(xb, wup_ref[:, c:c + FF_CHUNK]), 0.0)
        y = y + _dot((h * h).astype(BF16), wdown_ref[c:c + FF_CHUNK, :])
    y_ref[...] = _layernorm(y, g_ref[...], b_ref[...])


def _mlp(x2d, wup, wdown, g, b, alpha, tm):
    tokens = x2d.shape[0]
    tok = pl.BlockSpec((tm, D_MODEL), lambda i: (i, 0))
    vec = _resident((1, D_MODEL))
    return pl.pallas_call(
        functools.partial(_mlp_kernel, alpha=alpha),
        grid=(tokens // tm,),
        in_specs=[tok, _resident((D_MODEL, D_FF)), _resident((D_FF, D_MODEL)), vec, vec],
        out_specs=tok,
        out_shape=jax.ShapeDtypeStruct((tokens, D_MODEL), F32),
        compiler_params=_params(("parallel",)),
        name="mlp",
    )(x2d, wup, wdown, g, b)


def _tile(n, want):
    t = min(n, want)
    assert n % t == 0
    return t


def _layer(x, mem, w, alpha):
    bsz, seq_len, _ = x.shape
    x2d = x.reshape(bsz * seq_len, D_MODEL)
    tm = _tile(seq_len, 512)
    qt, k, vt, zc, zs = _in_projection(x2d, bsz, seq_len, w["qkvu"], w["qk_gain"], tm)
    o = _attention(qt, k.reshape(bsz, seq_len, KV_W), vt, _tile(seq_len, 256), _tile(seq_len, 512))
    n1, _ = _dft_factors(seq_len)
    f = _sequence_dft(zc, zs, bsz, seq_len, _tile(n1, 8), _tile(n1, 8))
    km, vm = _mem_kv(mem.reshape(bsz * MEM_TOKENS, D_MODEL), w["mem_kv"])
    x2 = _mix_xattn(x2d, o.reshape(bsz * seq_len, ATTN_W), f, seq_len, w["gate"], w["attn_branch"],
                    w["fourier_branch"], w["mix_out"], w["ln1_g"], w["ln1_b"], w["mem_q"], km, vm,
                    w["mem_o"], w["ln2_g"], w["ln2_b"], alpha, tm)
    y = _mlp(x2, w["up"], w["down"], w["ln3_g"], w["ln3_b"], alpha, tm)
    return y.reshape(bsz, seq_len, D_MODEL)


def kernel(x_prompt, x_sample, mem_prompt, mem_sample, w_in, q_norm, k_norm, w_attn_branch, w_fourier_branch, w_mix_out, ln1_g, ln1_b, w_mem_q, w_mem_k, w_mem_v, w_mem_o, ln2_g, ln2_b, w_up, w_down, ln3_g, ln3_b):
    depth = w_in.shape[0]
    alpha = float((2 * depth) ** 0.25)
    y_prompt, y_sample = x_prompt, x_sample
    for l in range(depth):
        row = lambda v: v[l].reshape(1, -1).astype(F32)
        w = {
            "qkvu": w_in[l, :, :QKVU_W].astype(BF16),
            "gate": w_in[l, :, QKVU_W:].astype(BF16),
            "qk_gain": jnp.concatenate([jnp.tile(q_norm[l], N_HEADS), jnp.tile(k_norm[l], N_KV_HEADS)]).reshape(1, -1).astype(F32),
            "attn_branch": w_attn_branch[l].astype(BF16),
            "fourier_branch": w_fourier_branch[l].astype(BF16),
            "mix_out": w_mix_out[l].astype(BF16),
            "ln1_g": row(ln1_g), "ln1_b": row(ln1_b),
            "mem_q": w_mem_q[l].astype(BF16),
            "mem_kv": jnp.concatenate([w_mem_k[l], w_mem_v[l]], axis=1).astype(BF16),
            "mem_o": w_mem_o[l].astype(BF16),
            "ln2_g": row(ln2_g), "ln2_b": row(ln2_b),
            "up": w_up[l].astype(BF16), "down": w_down[l].astype(BF16),
            "ln3_g": row(ln3_g), "ln3_b": row(ln3_b),
        }
        y_prompt = _layer(y_prompt, mem_prompt, w, alpha)
        y_sample = _layer(y_sample, mem_sample, w, alpha)
    return (y_prompt, y_sample)
```

```python
import functools
import math

import numpy as np
import jax
import jax.numpy as jnp
from jax import lax
from jax.experimental import pallas as pl
from jax.experimental.pallas import tpu as pltpu

D_MODEL = 1024
GRID_W = 64
N_HEADS = 8
N_KV_HEADS = 2
HEAD_DIM = 64
GROUP = N_HEADS // N_KV_HEADS
ATTN_W = N_HEADS * HEAD_DIM
KV_W = N_KV_HEADS * HEAD_DIM
ROPE_THETA = 10000.0
AXIS_ROT = HEAD_DIM // 2
FOURIER_GROUPS = 4
FOURIER_GROUP_W = 128
FOURIER_W = FOURIER_GROUPS * FOURIER_GROUP_W
QKVU_W = ATTN_W + 2 * KV_W + FOURIER_W
MEM_TOKENS = 256
MEM_HEADS = 4
MEM_HEAD_DIM = D_MODEL // MEM_HEADS
D_FF = 4 * D_MODEL
RMS_EPS = 1e-6
LN_EPS = 1e-5
LOG2E = math.log2(math.e)

V7X_VMEM_BYTES = 64 * 1024 * 1024
VMEM_LIMIT_BYTES = V7X_VMEM_BYTES - 12 * 1024 * 1024
LANES = 128

BF16 = jnp.bfloat16
F32 = jnp.float32


def _dot(a, b):
    return jnp.dot(a, b, preferred_element_type=F32)


def _dot_nt(a, b):
    return lax.dot_general(a, b, (((1,), (1,)), ((), ())), preferred_element_type=F32)


def _resident(shape):
    n = len(shape)
    return pl.BlockSpec(shape, lambda *_: (0,) * n, pipeline_mode=pl.Buffered(1))


def _params(semantics):
    return pltpu.CompilerParams(dimension_semantics=semantics, vmem_limit_bytes=VMEM_LIMIT_BYTES)


def _layernorm(x, g, b):
    mu = jnp.mean(x, axis=-1, keepdims=True)
    xc = x - mu
    var = jnp.mean(xc * xc, axis=-1, keepdims=True)
    return xc * lax.rsqrt(var + LN_EPS) * g + b


def _inproj_kernel(x_ref, w_ref, gain_ref, cos_ref, sin_ref, seg_ref, cdft_ref,
                   qt_ref, k_ref, vt_ref, zc_ref, zs_ref):
    xb = x_ref[...].astype(BF16)
    h = _dot(xb, w_ref[...])
    qk_w = ATTN_W + KV_W
    qk = h[:, :qk_w]
    sq = (qk * qk).astype(BF16)
    seg = seg_ref[...]
    ms = jnp.concatenate(
        [_dot(sq[:, c:c + 2 * LANES], seg) for c in range(0, ATTN_W, 2 * LANES)]
        + [_dot(sq[:, ATTN_W:qk_w], seg[:KV_W, :KV_W])], axis=-1)
    qkn = qk * lax.rsqrt(ms + RMS_EPS) * gain_ref[...]
    cos = cos_ref[...]
    sin = sin_ref[...]
    lane = lax.broadcasted_iota(jnp.int32, cos.shape, 1)
    first_half = (lane % HEAD_DIM) < (HEAD_DIM // 2)
    slabs = []
    for c in range(0, qk_w, LANES):
        xs = qkn[:, c:c + LANES]
        ahead = pltpu.roll(xs, LANES - HEAD_DIM // 2, 1)
        behind = pltpu.roll(xs, HEAD_DIM // 2, 1)
        slabs.append(xs * cos + jnp.where(first_half, ahead, behind) * sin)
    q = jnp.concatenate(slabs[:ATTN_W // LANES], axis=-1) * (HEAD_DIM ** -0.5 * LOG2E)
    qt_ref[...] = q.T.astype(BF16)
    k_ref[...] = slabs[-1].astype(BF16)
    vt_ref[...] = h[:, qk_w:qk_w + KV_W].T.astype(BF16)
    ub = h[:, qk_w + KV_W:].astype(BF16)
    cdft = cdft_ref[...]
    for gi in range(FOURIER_GROUPS):
        c = gi * FOURIER_GROUP_W
        z = _dot(ub[:, c:c + FOURIER_GROUP_W], cdft)
        zc_ref[:, c:c + FOURIER_GROUP_W] = z[:, :FOURIER_GROUP_W].astype(BF16)
        zs_ref[:, c:c + FOURIER_GROUP_W] = z[:, FOURIER_GROUP_W:].astype(BF16)


def _rope_tables(seq_len):
    rows = seq_len // GRID_W
    row = jnp.repeat(jnp.arange(rows, dtype=F32), GRID_W)
    col = jnp.tile(jnp.arange(GRID_W, dtype=F32), rows)
    freqs = ROPE_THETA ** (-jnp.arange(0, AXIS_ROT, 2, dtype=F32) / AXIS_ROT)
    ang = jnp.concatenate([row[:, None] * freqs, col[:, None] * freqs], axis=-1)
    ang = jnp.concatenate([ang, ang], axis=-1)
    cos, sin = jnp.cos(ang), jnp.sin(ang)
    sign = jnp.where(jnp.arange(HEAD_DIM) < HEAD_DIM // 2, -1.0, 1.0).astype(F32)
    reps = LANES // HEAD_DIM
    return jnp.tile(cos, (1, reps)), jnp.tile(sin * sign, (1, reps))


def _segment_mean_matrix():
    idx = np.arange(2 * LANES)
    m = (idx[:, None] // HEAD_DIM == idx[None, :] // HEAD_DIM).astype(np.float32) / HEAD_DIM
    return jnp.asarray(m, dtype=BF16)


def _channel_dft_matrix():
    n = FOURIER_GROUP_W
    jk = np.outer(np.arange(n), np.arange(n)) % n
    ang = 2.0 * np.pi * jk / n
    m = np.concatenate([np.cos(ang), np.sin(ang)], axis=1) / math.sqrt(n)
    return jnp.asarray(m, dtype=BF16)


def _in_projection(x2d, bsz, seq_len, w_qkvu, gain, tm):
    tokens = x2d.shape[0]
    assert seq_len % tm == 0
    tiles_per_seq = seq_len // tm
    cos, sin = _rope_tables(seq_len)
    tok = lambda i: (i, 0)
    pos = lambda i: (i % tiles_per_seq, 0)
    tr = lambda i: (i // tiles_per_seq, 0, i % tiles_per_seq)
    return pl.pallas_call(
        _inproj_kernel,
        grid=(tokens // tm,),
        in_specs=[
            pl.BlockSpec((tm, D_MODEL), tok),
            _resident((D_MODEL, QKVU_W)),
            _resident((1, ATTN_W + KV_W)),
            pl.BlockSpec((tm, LANES), pos),
            pl.BlockSpec((tm, LANES), pos),
            _resident((2 * LANES, 2 * LANES)),
            _resident((FOURIER_GROUP_W, 2 * FOURIER_GROUP_W)),
        ],
        out_specs=[
            pl.BlockSpec((None, ATTN_W, tm), tr),
            pl.BlockSpec((tm, KV_W), tok),
            pl.BlockSpec((None, KV_W, tm), tr),
            pl.BlockSpec((tm, FOURIER_W), tok),
            pl.BlockSpec((tm, FOURIER_W), tok),
        ],
        out_shape=[
            jax.ShapeDtypeStruct((bsz, ATTN_W, seq_len), BF16),
            jax.ShapeDtypeStruct((tokens, KV_W), BF16),
            jax.ShapeDtypeStruct((bsz, KV_W, seq_len), BF16),
            jax.ShapeDtypeStruct((tokens, FOURIER_W), BF16),
            jax.ShapeDtypeStruct((tokens, FOURIER_W), BF16),
        ],
        compiler_params=_params(("parallel",)),
        name="in_projection",
    )(x2d, w_qkvu, gain, cos, sin, _segment_mean_matrix(), _channel_dft_matrix())


SUM_ROWS = 16
KEY_SUBTILE = 256


def _attention_kernel(qt_ref, k_ref, vt_ref, o_ref, qpad_ref, sa_ref, sb_ref, acc_ref, *, tk, unroll):
    tq = qt_ref.shape[-1]
    seq_len = k_ref.shape[0]
    n_chunks = seq_len // tk
    assert unroll % 2 == 0 and n_chunks % unroll == 0 and tk % KEY_SUBTILE == 0
    kv_head = pl.program_id(1)
    row_head = lax.broadcasted_iota(jnp.int32, (KV_W, tq), 0) // HEAD_DIM
    for h in range(GROUP):
        qh = qt_ref[h * HEAD_DIM:(h + 1) * HEAD_DIM, :]
        q_rep = jnp.concatenate([qh] * N_KV_HEADS, axis=0)
        qpad_ref[h] = jnp.where(row_head == kv_head, q_rep, jnp.zeros_like(q_rep))
    acc_ref[...] = jnp.zeros_like(acc_ref)
    ones = jnp.ones((SUM_ROWS, tk), BF16)

    def key_chunk(c):
        return pl.ds(pl.multiple_of(c * tk, tk), tk)

    def step(c_next, s_next_ref, c_cur, s_cur_ref, m_run, m_chunk):
        if c_next is not None:
            kc = k_ref[key_chunk(c_next), :]
        if c_cur is not None:
            vc = jnp.concatenate([vt_ref[:, key_chunk(c_cur)], ones], axis=0)
        m_next, maxima = [], []
        for h in range(GROUP):
            if c_cur is not None:
                m_new = jnp.maximum(m_run[h], m_chunk[h])
                alpha = jnp.exp2(m_run[h] - m_new)
                m_next.append(m_new)
            pv, mx = None, None
            for t in range(0, tk, KEY_SUBTILE):
                if c_next is not None:
                    s = _dot(kc[t:t + KEY_SUBTILE], qpad_ref[h])
                    s_next_ref[h, t:t + KEY_SUBTILE, :] = s
                    smax = jnp.max(s, axis=0, keepdims=True)
                    mx = smax if mx is None else jnp.maximum(mx, smax)
                if c_cur is not None:
                    p = jnp.exp2(s_cur_ref[h, t:t + KEY_SUBTILE, :] - m_new).astype(BF16)
                    d = _dot(vc[:, t:t + KEY_SUBTILE], p)
                    pv = d if pv is None else pv + d
            if c_cur is not None:
                acc_ref[h] = alpha * acc_ref[h] + pv
            maxima.append(mx)
        return tuple(m_next), tuple(maxima)

    bufs = (sa_ref, sb_ref)

    def body(j, carry):
        m_run, m_chunk = carry
        for u in range(unroll):
            c = unroll * j + u
            m_run, m_chunk = step(c + 1, bufs[(u + 1) % 2], c, bufs[u % 2], m_run, m_chunk)
        return m_run, m_chunk

    m0 = tuple(jnp.full((1, tq), -jnp.inf, F32) for _ in range(GROUP))
    _, m_chunk = step(0, sa_ref, None, None, None, None)
    n_loop = n_chunks // unroll - 1
    m_run, m_chunk = lax.fori_loop(0, n_loop, body, (m0, m_chunk))
    for c in range(n_loop * unroll, n_chunks - 1):
        m_run, m_chunk = step(c + 1, bufs[(c + 1) % 2], c, bufs[c % 2], m_run, m_chunk)
    step(None, None, n_chunks - 1, bufs[(n_chunks - 1) % 2], m_run, m_chunk)
    outs = []
    for h in range(GROUP):
        a = acc_ref[h]
        outs.append(a[:HEAD_DIM] / a[HEAD_DIM:HEAD_DIM + 1])
    o_ref[...] = jnp.concatenate(outs, axis=0).T.astype(BF16)


def _attention(qt, k, vt, tq, tk, unroll):
    bsz, _, seq_len = qt.shape
    gw = GROUP * HEAD_DIM
    return pl.pallas_call(
        functools.partial(_attention_kernel, tk=tk, unroll=unroll),
        grid=(bsz, N_KV_HEADS, seq_len // tq),
        in_specs=[
            pl.BlockSpec((None, gw, tq), lambda b, g, i: (b, g, i)),
            pl.BlockSpec((None, seq_len, KV_W), lambda b, g, i: (b, 0, 0)),
            pl.BlockSpec((None, HEAD_DIM, seq_len), lambda b, g, i: (b, g, 0)),
        ],
        out_specs=pl.BlockSpec((None, tq, gw), lambda b, g, i: (b, i, g)),
        out_shape=jax.ShapeDtypeStruct((bsz, seq_len, ATTN_W), BF16),
        scratch_shapes=[pltpu.VMEM((GROUP, KV_W, tq), BF16),
                        pltpu.VMEM((GROUP, tk, tq), F32),
                        pltpu.VMEM((GROUP, tk, tq), F32),
                        pltpu.VMEM((GROUP, HEAD_DIM + SUM_ROWS, tq), F32)],
        compiler_params=_params(("parallel", "parallel", "parallel")),
        name="gqa_attention",
    )(qt, k, vt)


def _dft_stage1_kernel(zc_ref, zs_ref, m1_ref, twc_ref, tws_ref, yr_ref, yi_ref):
    n1 = zc_ref.shape[0]
    x = jnp.concatenate([zc_ref[...], zs_ref[...]], axis=0)
    y = _dot(m1_ref[...], x)
    yr, yi = y[:n1], y[n1:]
    c, s = twc_ref[...], tws_ref[...]
    yr_ref[...] = (yr * c + yi * s).astype(BF16)
    yi_ref[...] = (yi * c - yr * s).astype(BF16)


def _dft_stage2_kernel(yr_ref, yi_ref, m2_ref, f_ref):
    tk1 = yr_ref.shape[0]
    m2 = m2_ref[...]
    for j in range(tk1):
        y = jnp.concatenate([yr_ref[j], yi_ref[j]], axis=0)
        f_ref[:, j * FOURIER_W:(j + 1) * FOURIER_W] = _dot(m2, y).astype(BF16)


def _dft_factors(seq_len):
    n1 = 1 << (int(math.log2(seq_len)) // 2)
    n2 = seq_len // n1
    assert n1 * n2 == seq_len
    return n1, n2


def _dft_matrices(n1, n2):
    a1 = 2.0 * np.pi * (np.outer(np.arange(n1), np.arange(n1)) % n1) / n1
    c1, s1 = np.cos(a1) / math.sqrt(n1), np.sin(a1) / math.sqrt(n1)
    m1 = np.block([[c1, -s1], [-s1, -c1]])
    a2 = 2.0 * np.pi * (np.outer(np.arange(n2), np.arange(n2)) % n2) / n2
    m2 = np.concatenate([np.cos(a2), np.sin(a2)], axis=1) / math.sqrt(n2)
    return jnp.asarray(m1, dtype=BF16), jnp.asarray(m2, dtype=BF16)


def _twiddles(n1, n2):
    seq_len = n1 * n2
    k1 = jnp.arange(n1, dtype=jnp.int32)[:, None]
    nn = jnp.arange(n2, dtype=jnp.int32)[None, :]
    ang = ((k1 * nn) % seq_len).astype(F32) * (2.0 * math.pi / seq_len)
    rep = lambda t: jnp.repeat(t, FOURIER_W, axis=1)
    return rep(jnp.cos(ang)), rep(jnp.sin(ang))


def _sequence_dft(zc, zs, bsz, seq_len, tn2, tk1):
    n1, n2 = _dft_factors(seq_len)
    m1, m2 = _dft_matrices(n1, n2)
    twc, tws = _twiddles(n1, n2)
    width = n2 * FOURIER_W
    blk = tn2 * FOURIER_W
    data = pl.BlockSpec((None, n1, blk), lambda j, b: (b, 0, j))
    tw = pl.BlockSpec((n1, blk), lambda j, b: (0, j))
    yr, yi = pl.pallas_call(
        _dft_stage1_kernel,
        grid=(n2 // tn2, bsz),
        in_specs=[data, data, _resident((2 * n1, 2 * n1)), tw, tw],
        out_specs=[data, data],
        out_shape=[jax.ShapeDtypeStruct((bsz, n1, width), BF16)] * 2,
        compiler_params=_params(("parallel", "parallel")),
        name="dft_stage1",
    )(zc.reshape(bsz, n1, width), zs.reshape(bsz, n1, width), m1, twc, tws)
    planes = pl.BlockSpec((None, tk1, n2, FOURIER_W), lambda b, j: (b, j, 0, 0))
    f = pl.pallas_call(
        _dft_stage2_kernel,
        grid=(bsz, n1 // tk1),
        in_specs=[planes, planes, _resident((n2, 2 * n2))],
        out_specs=pl.BlockSpec((None, n2, tk1 * FOURIER_W), lambda b, j: (b, 0, j)),
        out_shape=jax.ShapeDtypeStruct((bsz, n2, n1 * FOURIER_W), BF16),
        compiler_params=_params(("parallel", "parallel")),
        name="dft_stage2",
    )(yr.reshape(bsz, n1, n2, FOURIER_W), yi.reshape(bsz, n1, n2, FOURIER_W), m2)
    return f.reshape(bsz * seq_len, FOURIER_W)


def _mem_kv_kernel(mem_ref, w_ref, k_ref, v_ref):
    y = _dot(mem_ref[...].astype(BF16), w_ref[...])
    k_ref[...] = y[:, :D_MODEL].astype(BF16)
    v_ref[...] = y[:, D_MODEL:].astype(BF16)


def _mem_kv(mem2d, w_kv):
    rows = mem2d.shape[0]
    blk = pl.BlockSpec((MEM_TOKENS, D_MODEL), lambda i: (i, 0))
    return pl.pallas_call(
        _mem_kv_kernel,
        grid=(rows // MEM_TOKENS,),
        in_specs=[blk, _resident((D_MODEL, 2 * D_MODEL))],
        out_specs=[blk, blk],
        out_shape=[jax.ShapeDtypeStruct((rows, D_MODEL), BF16)] * 2,
        compiler_params=_params(("parallel",)),
        name="mem_kv",
    )(mem2d, w_kv)


def _mix_xattn_kernel(x_ref, o_ref, f_ref, wg_ref, wab_ref, wfb_ref, wmix_ref, ln1g_ref, ln1b_ref,
                      wq_ref, km_ref, vm_ref, wo_ref, ln2g_ref, ln2b_ref, y_ref, *, alpha):
    x = x_ref[...]
    gates = _dot(x.astype(BF16), wg_ref[...])
    y_attn = _dot(o_ref[...], wab_ref[...])
    y_four = _dot(f_ref[...], wfb_ref[...])
    merged = (jax.nn.sigmoid(gates[:, :D_MODEL]) * y_attn
              + jax.nn.sigmoid(gates[:, D_MODEL:]) * y_four)
    x1 = _layernorm(alpha * x + _dot(merged.astype(BF16), wmix_ref[...]), ln1g_ref[...], ln1b_ref[...])
    q = (_dot(x1.astype(BF16), wq_ref[...]) * (MEM_HEAD_DIM ** -0.5 * LOG2E)).astype(BF16)
    heads = []
    for h in range(MEM_HEADS):
        c = h * MEM_HEAD_DIM
        s = _dot_nt(q[:, c:c + MEM_HEAD_DIM], km_ref[:, c:c + MEM_HEAD_DIM])
        p = jnp.exp2(s - jnp.max(s, axis=-1, keepdims=True))
        l = jnp.sum(p, axis=-1, keepdims=True)
        heads.append(_dot(p.astype(BF16), vm_ref[:, c:c + MEM_HEAD_DIM]) / l)
    ctx = jnp.concatenate(heads, axis=-1).astype(BF16)
    y_ref[...] = _layernorm(alpha * x1 + _dot(ctx, wo_ref[...]), ln2g_ref[...], ln2b_ref[...])


def _mix_xattn(x2d, o2d, f2d, seq_len, wg, wab, wfb, wmix, ln1g, ln1b, wq, km, vm, wo, ln2g, ln2b, alpha, tm):
    tokens = x2d.shape[0]
    assert seq_len % tm == 0
    tiles_per_seq = seq_len // tm
    tok = lambda w: pl.BlockSpec((tm, w), lambda i: (i, 0))
    mem = pl.BlockSpec((MEM_TOKENS, D_MODEL), lambda i: (i // tiles_per_seq, 0))
    vec = _resident((1, D_MODEL))
    return pl.pallas_call(
        functools.partial(_mix_xattn_kernel, alpha=alpha),
        grid=(tokens // tm,),
        in_specs=[tok(D_MODEL), tok(ATTN_W), tok(FOURIER_W),
                  _resident((D_MODEL, 2 * D_MODEL)), _resident((ATTN_W, D_MODEL)),
                  _resident((FOURIER_W, D_MODEL)), _resident((D_MODEL, D_MODEL)), vec, vec,
                  _resident((D_MODEL, D_MODEL)), mem, mem, _resident((D_MODEL, D_MODEL)), vec, vec],
        out_specs=tok(D_MODEL),
        out_shape=jax.ShapeDtypeStruct((tokens, D_MODEL), F32),
        compiler_params=_params(("parallel",)),
        name="mix_xattn",
    )(x2d, o2d, f2d, wg, wab, wfb, wmix, ln1g, ln1b, wq, km, vm, wo, ln2g, ln2b)


FF_CHUNK = 1024


def _mlp_kernel(x_ref, wup_ref, wdown_ref, g_ref, b_ref, y_ref, *, alpha):
    x = x_ref[...]
    xb = x.astype(BF16)
    y = alpha * x
    for c in range(0, D_FF, FF_CHUNK):
        h = jnp.maximum(_dot(xb, wup_ref[:, c:c + FF_CHUNK]), 0.0)
        y = y + _dot((h * h).astype(BF16), wdown_ref[c:c + FF_CHUNK, :])
    y_ref[...] = _layernorm(y, g_ref[...], b_ref[...])


def _mlp(x2d, wup, wdown, g, b, alpha, tm):
    tokens = x2d.shape[0]
    tok = pl.BlockSpec((tm, D_MODEL), lambda i: (i, 0))
    vec = _resident((1, D_MODEL))
    return pl.pallas_call(
        functools.partial(_mlp_kernel, alpha=alpha),
        grid=(tokens // tm,),
        in_specs=[tok, _resident((D_MODEL, D_FF)), _resident((D_FF, D_MODEL)), vec, vec],
        out_specs=tok,
        out_shape=jax.ShapeDtypeStruct((tokens, D_MODEL), F32),
        compiler_params=_params(("parallel",)),
        name="mlp",
    )(x2d, wup, wdown, g, b)


def _tile(n, want):
    t = min(n, want)
    assert n % t == 0
    return t


def _layer(x, mem, w, alpha):
    bsz, seq_len, _ = x.shape
    x2d = x.reshape(bsz * seq_len, D_MODEL)
    tm = _tile(seq_len, 512)
    qt, k, vt, zc, zs = _in_projection(x2d, bsz, seq_len, w["qkvu"], w["qk_gain"], tm)
    o = _attention(qt, k.reshape(bsz, seq_len, KV_W), vt, _tile(seq_len, 256), _tile(seq_len, 512),
                   unroll=4 if seq_len % 2048 == 0 else 2)
    n1, _ = _dft_factors(seq_len)
    f = _sequence_dft(zc, zs, bsz, seq_len, _tile(n1, 8), _tile(n1, 8))
    km, vm = _mem_kv(mem.reshape(bsz * MEM_TOKENS, D_MODEL), w["mem_kv"])
    x2 = _mix_xattn(x2d, o.reshape(bsz * seq_len, ATTN_W), f, seq_len, w["gate"], w["attn_branch"],
                    w["fourier_branch"], w["mix_out"], w["ln1_g"], w["ln1_b"], w["mem_q"], km, vm,
                    w["mem_o"], w["ln2_g"], w["ln2_b"], alpha, tm)
    y = _mlp(x2, w["up"], w["down"], w["ln3_g"], w["ln3_b"], alpha, tm)
    return y.reshape(bsz, seq_len, D_MODEL)


def kernel(x_prompt, x_sample, mem_prompt, mem_sample, w_in, q_norm, k_norm, w_attn_branch, w_fourier_branch, w_mix_out, ln1_g, ln1_b, w_mem_q, w_mem_k, w_mem_v, w_mem_o, ln2_g, ln2_b, w_up, w_down, ln3_g, ln3_b):
    depth = w_in.shape[0]
    alpha = float((2 * depth) ** 0.25)
    y_prompt, y_sample = x_prompt, x_sample
    for l in range(depth):
        row = lambda v: v[l].reshape(1, -1).astype(F32)
        w = {
            "qkvu": w_in[l, :, :QKVU_W].astype(BF16),
            "gate": w_in[l, :, QKVU_W:].astype(BF16),
            "qk_gain": jnp.concatenate([jnp.tile(q_norm[l], N_HEADS), jnp.tile(k_norm[l], N_KV_HEADS)]).reshape(1, -1).astype(F32),
            "attn_branch": w_attn_branch[l].astype(BF16),
            "fourier_branch": w_fourier_branch[l].astype(BF16),
            "mix_out": w_mix_out[l].astype(BF16),
            "ln1_g": row(ln1_g), "ln1_b": row(ln1_b),
            "mem_q": w_mem_q[l].astype(BF16),
            "mem_kv": jnp.concatenate([w_mem_k[l], w_mem_v[l]], axis=1).astype(BF16),
            "mem_o": w_mem_o[l].astype(BF16),
            "ln2_g": row(ln2_g), "ln2_b": row(ln2_b),
            "up": w_up[l].astype(BF16), "down": w_down[l].astype(BF16),
            "ln3_g": row(ln3_g), "ln3_b": row(ln3_b),
        }
        y_prompt = _layer(y_prompt, mem_prompt, w, alpha)
        y_sample = _layer(y_sample, mem_sample, w, alpha)
    return (y_prompt, y_sample)
```

```python
import functools
import math

import numpy as np
import jax
import jax.numpy as jnp
from jax import lax
from jax.experimental import pallas as pl
from jax.experimental.pallas import tpu as pltpu

D_MODEL = 1024
GRID_W = 64
N_HEADS = 8
N_KV_HEADS = 2
HEAD_DIM = 64
GROUP = N_HEADS // N_KV_HEADS
ATTN_W = N_HEADS * HEAD_DIM
KV_W = N_KV_HEADS * HEAD_DIM
ROPE_THETA = 10000.0
AXIS_ROT = HEAD_DIM // 2
FOURIER_GROUPS = 4
FOURIER_GROUP_W = 128
FOURIER_W = FOURIER_GROUPS * FOURIER_GROUP_W
QKVU_W = ATTN_W + 2 * KV_W + FOURIER_W
MEM_TOKENS = 256
MEM_HEADS = 4
MEM_HEAD_DIM = D_MODEL // MEM_HEADS
D_FF = 4 * D_MODEL
RMS_EPS = 1e-6
LN_EPS = 1e-5
LOG2E = math.log2(math.e)

V7X_VMEM_BYTES = 64 * 1024 * 1024
VMEM_LIMIT_BYTES = V7X_VMEM_BYTES - 12 * 1024 * 1024
LANES = 128

BF16 = jnp.bfloat16
F32 = jnp.float32


def _dot(a, b):
    return jnp.dot(a, b, preferred_element_type=F32)


def _dot_nt(a, b):
    return lax.dot_general(a, b, (((1,), (1,)), ((), ())), preferred_element_type=F32)


def _resident(shape):
    n = len(shape)
    return pl.BlockSpec(shape, lambda *_: (0,) * n, pipeline_mode=pl.Buffered(1))


def _params(semantics):
    return pltpu.CompilerParams(dimension_semantics=semantics, vmem_limit_bytes=VMEM_LIMIT_BYTES)


def _layernorm(x, g, b):
    mu = jnp.mean(x, axis=-1, keepdims=True)
    xc = x - mu
    var = jnp.mean(xc * xc, axis=-1, keepdims=True)
    return xc * lax.rsqrt(var + LN_EPS) * g + b


FAST = 8


def _grid_side(seq_len):
    n = math.isqrt(seq_len)
    assert n * n == seq_len and n % FAST == 0, "sequence length must be a square of a multiple of 8"
    return n


def _tile_view(width):
    return lambda n: pl.BlockSpec((None, n, FAST, width), lambda b, j: (b, 0, j, 0))


def _inproj_kernel(x_ref, w_ref, gain_ref, cos_ref, sin_ref, seg_ref, cdft_ref, m1_ref, twc_ref, tws_ref,
                   qt_ref, k_ref, vt_ref, yr_ref, yi_ref, zc_scr, zs_scr):
    n = x_ref.shape[0]
    tm = n * FAST
    xb = x_ref[...].reshape(tm, D_MODEL).astype(BF16)
    h = _dot(xb, w_ref[...])
    qk_w = ATTN_W + KV_W
    qk = h[:, :qk_w]
    sq = (qk * qk).astype(BF16)
    seg = seg_ref[...]
    ms = jnp.concatenate(
        [_dot(sq[:, c:c + 2 * LANES], seg) for c in range(0, ATTN_W, 2 * LANES)]
        + [_dot(sq[:, ATTN_W:qk_w], seg[:KV_W, :KV_W])], axis=-1)
    qkn = qk * lax.rsqrt(ms + RMS_EPS) * gain_ref[...]
    cos = cos_ref[...].reshape(tm, LANES)
    sin = sin_ref[...].reshape(tm, LANES)
    lane = lax.broadcasted_iota(jnp.int32, cos.shape, 1)
    first_half = (lane % HEAD_DIM) < (HEAD_DIM // 2)
    slabs = []
    for c in range(0, qk_w, LANES):
        xs = qkn[:, c:c + LANES]
        ahead = pltpu.roll(xs, LANES - HEAD_DIM // 2, 1)
        behind = pltpu.roll(xs, HEAD_DIM // 2, 1)
        slabs.append(xs * cos + jnp.where(first_half, ahead, behind) * sin)
    q = jnp.concatenate(slabs[:ATTN_W // LANES], axis=-1) * (HEAD_DIM ** -0.5 * LOG2E)
    qt_ref[...] = q.T.astype(BF16)
    k_ref[...] = slabs[-1].astype(BF16)
    vt_ref[...] = h[:, qk_w:qk_w + KV_W].T.astype(BF16)
    ub = h[:, qk_w + KV_W:].astype(BF16)
    cdft = cdft_ref[...]
    for gi in range(FOURIER_GROUPS):
        c = gi * FOURIER_GROUP_W
        z = _dot(ub[:, c:c + FOURIER_GROUP_W], cdft)
        zc_scr[gi] = z[:, :FOURIER_GROUP_W]
        zs_scr[gi] = z[:, FOURIER_GROUP_W:]
    m1 = m1_ref[...]
    reps = FOURIER_W // LANES
    for r in range(FAST):
        rows = pl.ds(r, n, stride=FAST)
        gather = lambda scr: jnp.concatenate([scr.at[gi][rows, :] for gi in range(FOURIER_GROUPS)], axis=-1)
        x2 = jnp.concatenate([gather(zc_scr), gather(zs_scr)], axis=0).astype(BF16)
        y = _dot(m1, x2)
        yr, yi = y[:n], y[n:]
        c = jnp.concatenate([twc_ref[r]] * reps, axis=-1)
        s = jnp.concatenate([tws_ref[r]] * reps, axis=-1)
        yr_ref[:, r, :] = yr * c + yi * s
        yi_ref[:, r, :] = yi * c - yr * s


def _rope_tables(seq_len):
    rows = seq_len // GRID_W
    row = jnp.repeat(jnp.arange(rows, dtype=F32), GRID_W)
    col = jnp.tile(jnp.arange(GRID_W, dtype=F32), rows)
    freqs = ROPE_THETA ** (-jnp.arange(0, AXIS_ROT, 2, dtype=F32) / AXIS_ROT)
    ang = jnp.concatenate([row[:, None] * freqs, col[:, None] * freqs], axis=-1)
    ang = jnp.concatenate([ang, ang], axis=-1)
    cos, sin = jnp.cos(ang), jnp.sin(ang)
    sign = jnp.where(jnp.arange(HEAD_DIM) < HEAD_DIM // 2, -1.0, 1.0).astype(F32)
    reps = LANES // HEAD_DIM
    return jnp.tile(cos, (1, reps)), jnp.tile(sin * sign, (1, reps))


def _segment_mean_matrix():
    idx = np.arange(2 * LANES)
    m = (idx[:, None] // HEAD_DIM == idx[None, :] // HEAD_DIM).astype(np.float32) / HEAD_DIM
    return jnp.asarray(m, dtype=BF16)


def _channel_dft_matrix():
    n = FOURIER_GROUP_W
    jk = np.outer(np.arange(n), np.arange(n)) % n
    ang = 2.0 * np.pi * jk / n
    m = np.concatenate([np.cos(ang), np.sin(ang)], axis=1) / math.sqrt(n)
    return jnp.asarray(m, dtype=BF16)


def _dft_matrices(n):
    ang = 2.0 * np.pi * (np.outer(np.arange(n), np.arange(n)) % n) / n
    c, s = np.cos(ang) / math.sqrt(n), np.sin(ang) / math.sqrt(n)
    m1 = np.block([[c, -s], [-s, -c]])
    m2 = np.concatenate([c, s], axis=1)
    return jnp.asarray(m1, dtype=BF16), jnp.asarray(m2, dtype=BF16)


def _twiddles(n):
    seq_len = n * n
    idx = jnp.arange(n, dtype=jnp.int32)
    ang = ((idx[:, None] * idx[None, :]) % seq_len).astype(F32) * (2.0 * math.pi / seq_len)
    rep = lambda t: jnp.broadcast_to(t[:, :, None], (n, n, LANES))
    return rep(jnp.cos(ang)), rep(jnp.sin(ang))


def _in_projection(x4, w_qkvu, gain, m1):
    bsz, n, _, _ = x4.shape
    seq_len, tm, tiles = n * n, n * FAST, n // FAST
    cos, sin = _rope_tables(seq_len)
    twc, tws = _twiddles(n)
    pos = pl.BlockSpec((n, FAST, LANES), lambda b, j: (0, j, 0))
    tw = pl.BlockSpec((FAST, n, LANES), lambda b, j: (j, 0, 0))
    tr = lambda rows: pl.BlockSpec((None, rows, tm), lambda b, j: (b, 0, j))
    plane = _tile_view(FOURIER_W)(n)
    return pl.pallas_call(
        _inproj_kernel,
        grid=(bsz, tiles),
        in_specs=[
            _tile_view(D_MODEL)(n),
            _resident((D_MODEL, QKVU_W)),
            _resident((1, ATTN_W + KV_W)),
            pos, pos,
            _resident((2 * LANES, 2 * LANES)),
            _resident((FOURIER_GROUP_W, 2 * FOURIER_GROUP_W)),
            _resident((2 * n, 2 * n)),
            tw, tw,
        ],
        out_specs=[
            tr(ATTN_W),
            pl.BlockSpec((tm, KV_W), lambda b, j: (b * tiles + j, 0)),
            tr(KV_W),
            plane, plane,
        ],
        out_shape=[
            jax.ShapeDtypeStruct((bsz, ATTN_W, seq_len), BF16),
            jax.ShapeDtypeStruct((bsz * seq_len, KV_W), BF16),
            jax.ShapeDtypeStruct((bsz, KV_W, seq_len), BF16),
            jax.ShapeDtypeStruct((bsz, n, n, FOURIER_W), F32),
            jax.ShapeDtypeStruct((bsz, n, n, FOURIER_W), F32),
        ],
        scratch_shapes=[pltpu.VMEM((FOURIER_GROUPS, tm, FOURIER_GROUP_W), F32)] * 2,
        compiler_params=_params(("parallel", "parallel")),
        name="in_projection",
    )(x4, w_qkvu, gain, cos.reshape(n, n, LANES), sin.reshape(n, n, LANES),
      _segment_mean_matrix(), _channel_dft_matrix(), m1, twc, tws)


SUM_ROWS = 16
KEY_SUBTILE = 256


def _attention_kernel(qt_ref, k_ref, vt_ref, o_ref, qpad_ref, sa_ref, sb_ref, acc_ref, *, tk, unroll):
    tq = qt_ref.shape[-1]
    seq_len = k_ref.shape[0]
    n_chunks = seq_len // tk
    assert unroll % 2 == 0 and n_chunks % unroll == 0 and tk % KEY_SUBTILE == 0
    kv_head = pl.program_id(1)
    row_head = lax.broadcasted_iota(jnp.int32, (KV_W, tq), 0) // HEAD_DIM
    for h in range(GROUP):
        qh = qt_ref[h * HEAD_DIM:(h + 1) * HEAD_DIM, :]
        q_rep = jnp.concatenate([qh] * N_KV_HEADS, axis=0)
        qpad_ref[h] = jnp.where(row_head == kv_head, q_rep, jnp.zeros_like(q_rep))
    acc_ref[...] = jnp.zeros_like(acc_ref)
    ones = jnp.ones((SUM_ROWS, tk), BF16)

    def key_chunk(c):
        return pl.ds(pl.multiple_of(c * tk, tk), tk)

    def step(c_next, s_next_ref, c_cur, s_cur_ref, m_run, m_chunk):
        if c_next is not None:
            kc = k_ref[key_chunk(c_next), :]
        if c_cur is not None:
            vc = jnp.concatenate([vt_ref[:, key_chunk(c_cur)], ones], axis=0)
        m_next, maxima = [], []
        for h in range(GROUP):
            if c_cur is not None:
                m_new = jnp.maximum(m_run[h], m_chunk[h])
                alpha = jnp.exp2(m_run[h] - m_new)
                m_next.append(m_new)
            pv, mx = None, None
            for t in range(0, tk, KEY_SUBTILE):
                if c_next is not None:
                    s = _dot(kc[t:t + KEY_SUBTILE], qpad_ref[h])
                    s_next_ref[h, t:t + KEY_SUBTILE, :] = s
                    smax = jnp.max(s, axis=0, keepdims=True)
                    mx = smax if mx is None else jnp.maximum(mx, smax)
                if c_cur is not None:
                    p = jnp.exp2(s_cur_ref[h, t:t + KEY_SUBTILE, :] - m_new).astype(BF16)
                    d = _dot(vc[:, t:t + KEY_SUBTILE], p)
                    pv = d if pv is None else pv + d
            if c_cur is not None:
                acc_ref[h] = alpha * acc_ref[h] + pv
            maxima.append(mx)
        return tuple(m_next), tuple(maxima)

    bufs = (sa_ref, sb_ref)

    def body(j, carry):
        m_run, m_chunk = carry
        for u in range(unroll):
            c = unroll * j + u
            m_run, m_chunk = step(c + 1, bufs[(u + 1) % 2], c, bufs[u % 2], m_run, m_chunk)
        return m_run, m_chunk

    m0 = tuple(jnp.full((1, tq), -jnp.inf, F32) for _ in range(GROUP))
    _, m_chunk = step(0, sa_ref, None, None, None, None)
    n_loop = n_chunks // unroll - 1
    m_run, m_chunk = lax.fori_loop(0, n_loop, body, (m0, m_chunk))
    for c in range(n_loop * unroll, n_chunks - 1):
        m_run, m_chunk = step(c + 1, bufs[(c + 1) % 2], c, bufs[c % 2], m_run, m_chunk)
    step(None, None, n_chunks - 1, bufs[(n_chunks - 1) % 2], m_run, m_chunk)
    outs = []
    for h in range(GROUP):
        a = acc_ref[h]
        outs.append(a[:HEAD_DIM] / a[HEAD_DIM:HEAD_DIM + 1])
    o_ref[...] = jnp.concatenate(outs, axis=0).T.astype(BF16)


def _attention(qt, k, vt, tq, tk, unroll):
    bsz, _, seq_len = qt.shape
    gw = GROUP * HEAD_DIM
    return pl.pallas_call(
        functools.partial(_attention_kernel, tk=tk, unroll=unroll),
        grid=(bsz, N_KV_HEADS, seq_len // tq),
        in_specs=[
            pl.BlockSpec((None, gw, tq), lambda b, g, i: (b, g, i)),
            pl.BlockSpec((None, seq_len, KV_W), lambda b, g, i: (b, 0, 0)),
            pl.BlockSpec((None, HEAD_DIM, seq_len), lambda b, g, i: (b, g, 0)),
        ],
        out_specs=pl.BlockSpec((None, tq, gw), lambda b, g, i: (b, i, g)),
        out_shape=jax.ShapeDtypeStruct((bsz, seq_len, ATTN_W), BF16),
        scratch_shapes=[pltpu.VMEM((GROUP, KV_W, tq), BF16),
                        pltpu.VMEM((GROUP, tk, tq), F32),
                        pltpu.VMEM((GROUP, tk, tq), F32),
                        pltpu.VMEM((GROUP, HEAD_DIM + SUM_ROWS, tq), F32)],
        compiler_params=_params(("parallel", "parallel", "parallel")),
        name="gqa_attention",
    )(qt, k, vt)


def _mem_kv_kernel(mem_ref, w_ref, k_ref, v_ref):
    y = _dot(mem_ref[...].astype(BF16), w_ref[...])
    k_ref[...] = y[:, :D_MODEL].astype(BF16)
    v_ref[...] = y[:, D_MODEL:].astype(BF16)


def _mem_kv(mem2d, w_kv):
    rows = mem2d.shape[0]
    blk = pl.BlockSpec((MEM_TOKENS, D_MODEL), lambda i: (i, 0))
    return pl.pallas_call(
        _mem_kv_kernel,
        grid=(rows // MEM_TOKENS,),
        in_specs=[blk, _resident((D_MODEL, 2 * D_MODEL))],
        out_specs=[blk, blk],
        out_shape=[jax.ShapeDtypeStruct((rows, D_MODEL), BF16)] * 2,
        compiler_params=_params(("parallel",)),
        name="mem_kv",
    )(mem2d, w_kv)


MIX_ROWS = 512


def _mix_xattn_kernel(x_ref, o_ref, yr_ref, yi_ref, m2_ref, wg_ref, wab_ref, wfb_ref, wmix_ref,
                      ln1g_ref, ln1b_ref, wq_ref, km_ref, vm_ref, wo_ref, ln2g_ref, ln2b_ref,
                      y_ref, f_scr, *, alpha):
    n = x_ref.shape[0]
    tm = n * FAST
    m2 = m2_ref[...]
    for r in range(FAST):
        y2 = jnp.concatenate([yr_ref[r], yi_ref[r]], axis=0).astype(BF16)
        f = _dot(m2, y2)
        for gi in range(FOURIER_GROUPS):
            f_scr.at[gi][pl.ds(r, n, stride=FAST), :] = f[:, gi * FOURIER_GROUP_W:(gi + 1) * FOURIER_GROUP_W]

    pass_rows = min(MIX_ROWS, tm)

    def dense(i):
        rows = pl.ds(pl.multiple_of(i * pass_rows, pass_rows), pass_rows)
        slow = pl.ds(pl.multiple_of(i * (pass_rows // FAST), pass_rows // FAST), pass_rows // FAST)
        x = x_ref[slow].reshape(pass_rows, D_MODEL)
        gates = _dot(x.astype(BF16), wg_ref[...])
        y_attn = _dot(o_ref[rows, :], wab_ref[...])
        f = jnp.concatenate([f_scr[gi, rows, :] for gi in range(FOURIER_GROUPS)], axis=-1)
        y_four = _dot(f.astype(BF16), wfb_ref[...])
        merged = (jax.nn.sigmoid(gates[:, :D_MODEL]) * y_attn
                  + jax.nn.sigmoid(gates[:, D_MODEL:]) * y_four)
        x1 = _layernorm(alpha * x + _dot(merged.astype(BF16), wmix_ref[...]), ln1g_ref[...], ln1b_ref[...])
        q = (_dot(x1.astype(BF16), wq_ref[...]) * (MEM_HEAD_DIM ** -0.5 * LOG2E)).astype(BF16)
        heads = []
        for h in range(MEM_HEADS):
            c = h * MEM_HEAD_DIM
            s = _dot_nt(q[:, c:c + MEM_HEAD_DIM], km_ref[:, c:c + MEM_HEAD_DIM])
            p = jnp.exp2(s - jnp.max(s, axis=-1, keepdims=True))
            l = jnp.sum(p, axis=-1, keepdims=True)
            heads.append(_dot(p.astype(BF16), vm_ref[:, c:c + MEM_HEAD_DIM]) / l)
        ctx = jnp.concatenate(heads, axis=-1).astype(BF16)
        y_ref[rows, :] = _layernorm(alpha * x1 + _dot(ctx, wo_ref[...]), ln2g_ref[...], ln2b_ref[...])

    assert tm % pass_rows == 0
    if tm == pass_rows:
        dense(0)
    else:
        def body(i, carry):
            dense(i)
            return carry
        lax.fori_loop(0, tm // pass_rows, body, 0)


def _mix_xattn(x4, o2d, yr, yi, m2, wg, wab, wfb, wmix, ln1g, ln1b, wq, km, vm, wo, ln2g, ln2b, alpha):
    bsz, n, _, _ = x4.shape
    tm, tiles = n * FAST, n // FAST
    tok = lambda w: pl.BlockSpec((tm, w), lambda b, j: (b * tiles + j, 0))
    plane = pl.BlockSpec((None, FAST, n, FOURIER_W), lambda b, j: (b, j, 0, 0))
    mem = pl.BlockSpec((MEM_TOKENS, D_MODEL), lambda b, j: (b, 0))
    vec = _resident((1, D_MODEL))
    return pl.pallas_call(
        functools.partial(_mix_xattn_kernel, alpha=alpha),
        grid=(bsz, tiles),
        in_specs=[_tile_view(D_MODEL)(n), tok(ATTN_W), plane, plane, _resident((n, 2 * n)),
                  _resident((D_MODEL, 2 * D_MODEL)), _resident((ATTN_W, D_MODEL)),
                  _resident((FOURIER_W, D_MODEL)), _resident((D_MODEL, D_MODEL)), vec, vec,
                  _resident((D_MODEL, D_MODEL)), mem, mem, _resident((D_MODEL, D_MODEL)), vec, vec],
        out_specs=tok(D_MODEL),
        out_shape=jax.ShapeDtypeStruct((bsz * n * n, D_MODEL), F32),
        scratch_shapes=[pltpu.VMEM((FOURIER_GROUPS, tm, FOURIER_GROUP_W), F32)],
        compiler_params=_params(("parallel", "parallel")),
        name="mix_xattn",
    )(x4, o2d, yr, yi, m2, wg, wab, wfb, wmix, ln1g, ln1b, wq, km, vm, wo, ln2g, ln2b)


FF_CHUNK = 1024


def _mlp_kernel(x_ref, wup_ref, wdown_ref, g_ref, b_ref, y_ref, *, alpha):
    x = x_ref[...]
    xb = x.astype(BF16)
    y = alpha * x
    for c in range(0, D_FF, FF_CHUNK):
        h = jnp.maximum(_dot(xb, wup_ref[:, c:c + FF_CHUNK]), 0.0)
        y = y + _dot((h * h).astype(BF16), wdown_ref[c:c + FF_CHUNK, :])
    y_ref[...] = _layernorm(y, g_ref[...], b_ref[...]).reshape(y_ref.shape)


def _mlp(x2d, bsz, n, wup, wdown, g, b, alpha):
    tm, tiles = n * FAST, n // FAST
    vec = _resident((1, D_MODEL))
    return pl.pallas_call(
        functools.partial(_mlp_kernel, alpha=alpha),
        grid=(bsz, tiles),
        in_specs=[pl.BlockSpec((tm, D_MODEL), lambda b, j: (b * tiles + j, 0)),
                  _resident((D_MODEL, D_FF)), _resident((D_FF, D_MODEL)), vec, vec],
        out_specs=_tile_view(D_MODEL)(n),
        out_shape=jax.ShapeDtypeStruct((bsz, n, n, D_MODEL), F32),
        compiler_params=_params(("parallel", "parallel")),
        name="mlp",
    )(x2d, wup, wdown, g, b)


def _tile(n, want):
    t = min(n, want)
    assert n % t == 0
    return t


def _layer(x, mem, w, alpha):
    bsz, seq_len, _ = x.shape
    n = _grid_side(seq_len)
    x4 = x.reshape(bsz, n, n, D_MODEL)
    m1, m2 = _dft_matrices(n)
    qt, k, vt, yr, yi = _in_projection(x4, w["qkvu"], w["qk_gain"], m1)
    o = _attention(qt, k.reshape(bsz, seq_len, KV_W), vt, _tile(seq_len, 256), _tile(seq_len, 512),
                   unroll=4 if seq_len % 2048 == 0 else 2)
    km, vm = _mem_kv(mem.reshape(bsz * MEM_TOKENS, D_MODEL), w["mem_kv"])
    x2 = _mix_xattn(x4, o.reshape(bsz * seq_len, ATTN_W), yr, yi, m2, w["gate"], w["attn_branch"],
                    w["fourier_branch"], w["mix_out"], w["ln1_g"], w["ln1_b"], w["mem_q"], km, vm,
                    w["mem_o"], w["ln2_g"], w["ln2_b"], alpha)
    y = _mlp(x2, bsz, n, w["up"], w["down"], w["ln3_g"], w["ln3_b"], alpha)
    return y.reshape(bsz, seq_len, D_MODEL)


def kernel(x_prompt, x_sample, mem_prompt, mem_sample, w_in, q_norm, k_norm, w_attn_branch, w_fourier_branch, w_mix_out, ln1_g, ln1_b, w_mem_q, w_mem_k, w_mem_v, w_mem_o, ln2_g, ln2_b, w_up, w_down, ln3_g, ln3_b):
    depth = w_in.shape[0]
    alpha = float((2 * depth) ** 0.25)
    y_prompt, y_sample = x_prompt, x_sample
    for l in range(depth):
        row = lambda v: v[l].reshape(1, -1).astype(F32)
        w = {
            "qkvu": w_in[l, :, :QKVU_W].astype(BF16),
            "gate": w_in[l, :, QKVU_W:].astype(BF16),
            "qk_gain": jnp.concatenate([jnp.tile(q_norm[l], N_HEADS), jnp.tile(k_norm[l], N_KV_HEADS)]).reshape(1, -1).astype(F32),
            "attn_branch": w_attn_branch[l].astype(BF16),
            "fourier_branch": w_fourier_branch[l].astype(BF16),
            "mix_out": w_mix_out[l].astype(BF16),
            "ln1_g": row(ln1_g), "ln1_b": row(ln1_b),
            "mem_q": w_mem_q[l].astype(BF16),
            "mem_kv": jnp.concatenate([w_mem_k[l], w_mem_v[l]], axis=1).astype(BF16),
            "mem_o": w_mem_o[l].astype(BF16),
            "ln2_g": row(ln2_g), "ln2_b": row(ln2_b),
            "up": w_up[l].astype(BF16), "down": w_down[l].astype(BF16),
            "ln3_g": row(ln3_g), "ln3_b": row(ln3_b),
        }
        y_prompt = _layer(y_prompt, mem_prompt, w, alpha)
        y_sample = _layer(y_sample, mem_sample, w, alpha)
    return (y_prompt, y_sample)
```

```python
import functools
import math

import numpy as np
import jax
import jax.numpy as jnp
from jax import lax
from jax.experimental import pallas as pl
from jax.experimental.pallas import tpu as pltpu

D_MODEL = 1024
GRID_W = 64
N_HEADS = 8
N_KV_HEADS = 2
HEAD_DIM = 64
GROUP = N_HEADS // N_KV_HEADS
ATTN_W = N_HEADS * HEAD_DIM
KV_W = N_KV_HEADS * HEAD_DIM
ROPE_THETA = 10000.0
AXIS_ROT = HEAD_DIM // 2
FOURIER_GROUPS = 4
FOURIER_GROUP_W = 128
FOURIER_W = FOURIER_GROUPS * FOURIER_GROUP_W
QKVU_W = ATTN_W + 2 * KV_W + FOURIER_W
MEM_TOKENS = 256
MEM_HEADS = 4
MEM_HEAD_DIM = D_MODEL // MEM_HEADS
D_FF = 4 * D_MODEL
RMS_EPS = 1e-6
LN_EPS = 1e-5
LOG2E = math.log2(math.e)

V7X_VMEM_BYTES = 64 * 1024 * 1024
VMEM_LIMIT_BYTES = V7X_VMEM_BYTES - 12 * 1024 * 1024
LANES = 128

BF16 = jnp.bfloat16
F32 = jnp.float32


def _dot(a, b):
    return jnp.dot(a, b, preferred_element_type=F32)


def _dot_nt(a, b):
    return lax.dot_general(a, b, (((1,), (1,)), ((), ())), preferred_element_type=F32)


def _resident(shape):
    n = len(shape)
    return pl.BlockSpec(shape, lambda *_: (0,) * n, pipeline_mode=pl.Buffered(1))


def _params(semantics):
    return pltpu.CompilerParams(dimension_semantics=semantics, vmem_limit_bytes=VMEM_LIMIT_BYTES)


def _layernorm(x, g, b):
    mu = jnp.mean(x, axis=-1, keepdims=True)
    xc = x - mu
    var = jnp.mean(xc * xc, axis=-1, keepdims=True)
    return xc * lax.rsqrt(var + LN_EPS) * g + b


FAST = 8


def _grid_side(seq_len):
    n = math.isqrt(seq_len)
    assert n * n == seq_len and n % FAST == 0, "sequence length must be a square of a multiple of 8"
    return n


def _tile_view(width):
    return lambda n: pl.BlockSpec((None, n, FAST, width), lambda b, j: (b, 0, j, 0))


def _inproj_kernel(x_ref, w_ref, gain_ref, cos_ref, sin_ref, seg_ref, cdft_ref, m1_ref, twc_ref, tws_ref,
                   qt_ref, k_ref, vt_ref, yr_ref, yi_ref, zc_scr, zs_scr):
    n = x_ref.shape[0]
    tm = n * FAST
    xb = x_ref[...].reshape(tm, D_MODEL).astype(BF16)
    h = _dot(xb, w_ref[...])
    qk_w = ATTN_W + KV_W
    qk = h[:, :qk_w]
    sq = (qk * qk).astype(BF16)
    seg = seg_ref[...]
    ms = jnp.concatenate(
        [_dot(sq[:, c:c + 2 * LANES], seg) for c in range(0, ATTN_W, 2 * LANES)]
        + [_dot(sq[:, ATTN_W:qk_w], seg[:KV_W, :KV_W])], axis=-1)
    qkn = qk * lax.rsqrt(ms + RMS_EPS) * gain_ref[...]
    cos = cos_ref[...].reshape(tm, LANES)
    sin = sin_ref[...].reshape(tm, LANES)
    lane = lax.broadcasted_iota(jnp.int32, cos.shape, 1)
    first_half = (lane % HEAD_DIM) < (HEAD_DIM // 2)
    slabs = []
    for c in range(0, qk_w, LANES):
        xs = qkn[:, c:c + LANES]
        ahead = pltpu.roll(xs, LANES - HEAD_DIM // 2, 1)
        behind = pltpu.roll(xs, HEAD_DIM // 2, 1)
        slabs.append(xs * cos + jnp.where(first_half, ahead, behind) * sin)
    q = jnp.concatenate(slabs[:ATTN_W // LANES], axis=-1) * (HEAD_DIM ** -0.5 * LOG2E)
    qt_ref[...] = q.T.astype(BF16)
    k_ref[...] = slabs[-1].astype(BF16)
    vt_ref[...] = h[:, qk_w:qk_w + KV_W].T.astype(BF16)
    ub = h[:, qk_w + KV_W:].astype(BF16)
    cdft = cdft_ref[...]
    for gi in range(FOURIER_GROUPS):
        c = gi * FOURIER_GROUP_W
        z = _dot(ub[:, c:c + FOURIER_GROUP_W], cdft)
        zc_scr[gi] = z[:, :FOURIER_GROUP_W]
        zs_scr[gi] = z[:, FOURIER_GROUP_W:]
    m1 = m1_ref[...]
    reps = FOURIER_W // LANES
    for r in range(FAST):
        rows = pl.ds(r, n, stride=FAST)
        gather = lambda scr: jnp.concatenate([scr.at[gi][rows, :] for gi in range(FOURIER_GROUPS)], axis=-1)
        x2 = jnp.concatenate([gather(zc_scr), gather(zs_scr)], axis=0).astype(BF16)
        y = _dot(m1, x2)
        yr, yi = y[:n], y[n:]
        c = jnp.concatenate([twc_ref[r]] * reps, axis=-1)
        s = jnp.concatenate([tws_ref[r]] * reps, axis=-1)
        yr_ref[:, r, :] = yr * c + yi * s
        yi_ref[:, r, :] = yi * c - yr * s


def _rope_tables(seq_len):
    rows = seq_len // GRID_W
    row = jnp.repeat(jnp.arange(rows, dtype=F32), GRID_W)
    col = jnp.tile(jnp.arange(GRID_W, dtype=F32), rows)
    freqs = ROPE_THETA ** (-jnp.arange(0, AXIS_ROT, 2, dtype=F32) / AXIS_ROT)
    ang = jnp.concatenate([row[:, None] * freqs, col[:, None] * freqs], axis=-1)
    ang = jnp.concatenate([ang, ang], axis=-1)
    cos, sin = jnp.cos(ang), jnp.sin(ang)
    sign = jnp.where(jnp.arange(HEAD_DIM) < HEAD_DIM // 2, -1.0, 1.0).astype(F32)
    reps = LANES // HEAD_DIM
    return jnp.tile(cos, (1, reps)), jnp.tile(sin * sign, (1, reps))


def _segment_mean_matrix():
    idx = np.arange(2 * LANES)
    m = (idx[:, None] // HEAD_DIM == idx[None, :] // HEAD_DIM).astype(np.float32) / HEAD_DIM
    return jnp.asarray(m, dtype=BF16)


def _channel_dft_matrix():
    n = FOURIER_GROUP_W
    jk = np.outer(np.arange(n), np.arange(n)) % n
    ang = 2.0 * np.pi * jk / n
    m = np.concatenate([np.cos(ang), np.sin(ang)], axis=1) / math.sqrt(n)
    return jnp.asarray(m, dtype=BF16)


def _dft_matrices(n):
    ang = 2.0 * np.pi * (np.outer(np.arange(n), np.arange(n)) % n) / n
    c, s = np.cos(ang) / math.sqrt(n), np.sin(ang) / math.sqrt(n)
    m1 = np.block([[c, -s], [-s, -c]])
    m2 = np.concatenate([c, s], axis=1)
    return jnp.asarray(m1, dtype=BF16), jnp.asarray(m2, dtype=BF16)


def _twiddles(n):
    seq_len = n * n
    idx = jnp.arange(n, dtype=jnp.int32)
    ang = ((idx[:, None] * idx[None, :]) % seq_len).astype(F32) * (2.0 * math.pi / seq_len)
    rep = lambda t: jnp.broadcast_to(t[:, :, None], (n, n, LANES))
    return rep(jnp.cos(ang)), rep(jnp.sin(ang))


def _in_projection(x4, w_qkvu, gain, m1):
    bsz, n, _, _ = x4.shape
    seq_len, tm, tiles = n * n, n * FAST, n // FAST
    cos, sin = _rope_tables(seq_len)
    twc, tws = _twiddles(n)
    pos = pl.BlockSpec((n, FAST, LANES), lambda b, j: (0, j, 0))
    tw = pl.BlockSpec((FAST, n, LANES), lambda b, j: (j, 0, 0))
    tr = lambda rows: pl.BlockSpec((None, rows, tm), lambda b, j: (b, 0, j))
    plane = _tile_view(FOURIER_W)(n)
    return pl.pallas_call(
        _inproj_kernel,
        grid=(bsz, tiles),
        in_specs=[
            _tile_view(D_MODEL)(n),
            _resident((D_MODEL, QKVU_W)),
            _resident((1, ATTN_W + KV_W)),
            pos, pos,
            _resident((2 * LANES, 2 * LANES)),
            _resident((FOURIER_GROUP_W, 2 * FOURIER_GROUP_W)),
            _resident((2 * n, 2 * n)),
            tw, tw,
        ],
        out_specs=[
            tr(ATTN_W),
            pl.BlockSpec((tm, KV_W), lambda b, j: (b * tiles + j, 0)),
            tr(KV_W),
            plane, plane,
        ],
        out_shape=[
            jax.ShapeDtypeStruct((bsz, ATTN_W, seq_len), BF16),
            jax.ShapeDtypeStruct((bsz * seq_len, KV_W), BF16),
            jax.ShapeDtypeStruct((bsz, KV_W, seq_len), BF16),
            jax.ShapeDtypeStruct((bsz, n, n, FOURIER_W), F32),
            jax.ShapeDtypeStruct((bsz, n, n, FOURIER_W), F32),
        ],
        scratch_shapes=[pltpu.VMEM((FOURIER_GROUPS, tm, FOURIER_GROUP_W), F32)] * 2,
        compiler_params=_params(("parallel", "parallel")),
        name="in_projection",
    )(x4, w_qkvu, gain, cos.reshape(n, n, LANES), sin.reshape(n, n, LANES),
      _segment_mean_matrix(), _channel_dft_matrix(), m1, twc, tws)


SUM_ROWS = 16
KEY_SUBTILE = 256


def _attention_kernel(qt_ref, k_ref, vt_ref, o_ref, qpad_ref, sa_ref, sb_ref, acc_ref, *, tq, tk, unroll):
    n_tiles = qt_ref.shape[-1] // tq
    seq_len = k_ref.shape[0]
    n_chunks = seq_len // tk
    assert unroll % 2 == 0 and n_chunks % unroll == 0 and tk % KEY_SUBTILE == 0
    kv_head = pl.program_id(1)
    row_head = lax.broadcasted_iota(jnp.int32, (KV_W, tq), 0) // HEAD_DIM
    ones = jnp.ones((SUM_ROWS, tk), BF16)
    bufs = (sa_ref, sb_ref)

    def load_queries(tile, slot):
        cols = pl.ds(pl.multiple_of(tile * tq, tq), tq)
        for h in range(GROUP):
            qh = qt_ref[h * HEAD_DIM:(h + 1) * HEAD_DIM, cols]
            q_rep = jnp.concatenate([qh] * N_KV_HEADS, axis=0)
            qpad_ref[slot, h] = jnp.where(row_head == kv_head, q_rep, jnp.zeros_like(q_rep))

    def key_chunk(c):
        return pl.ds(pl.multiple_of(c * tk, tk), tk)

    def step(q_slot, c_next, s_next_ref, c_cur, s_cur_ref, m_run, m_chunk):
        kc = k_ref[key_chunk(c_next), :]
        if c_cur is not None:
            vc = jnp.concatenate([vt_ref[:, key_chunk(c_cur)], ones], axis=0)
        m_next, maxima = [], []
        for h in range(GROUP):
            if c_cur is not None:
                m_new = jnp.maximum(m_run[h], m_chunk[h])
                alpha = jnp.exp2(m_run[h] - m_new)
                m_next.append(m_new)
            pv, mx = None, None
            for t in range(0, tk, KEY_SUBTILE):
                s = _dot(kc[t:t + KEY_SUBTILE], qpad_ref[q_slot, h])
                s_next_ref[h, t:t + KEY_SUBTILE, :] = s
                smax = jnp.max(s, axis=0, keepdims=True)
                mx = smax if mx is None else jnp.maximum(mx, smax)
                if c_cur is not None:
                    p = jnp.exp2(s_cur_ref[h, t:t + KEY_SUBTILE, :] - m_new).astype(BF16)
                    d = _dot(vc[:, t:t + KEY_SUBTILE], p)
                    pv = d if pv is None else pv + d
            if c_cur is not None:
                acc_ref[h] = alpha * acc_ref[h] + pv
            maxima.append(mx)
        return tuple(m_next), tuple(maxima)

    def tile_body(tile, m_chunk):
        slot = tile % 2
        load_queries(jnp.minimum(tile + 1, n_tiles - 1), 1 - slot)
        acc_ref[...] = jnp.zeros_like(acc_ref)
        m_run = tuple(jnp.full((1, tq), -jnp.inf, F32) for _ in range(GROUP))

        def group(j, carry):
            m_run, m_chunk = carry
            for u in range(unroll):
                c = unroll * j + u
                m_run, m_chunk = step(slot, c + 1, bufs[(u + 1) % 2], c, bufs[u % 2], m_run, m_chunk)
            return m_run, m_chunk

        m_run, m_chunk = lax.fori_loop(0, n_chunks // unroll - 1, group, (m_run, m_chunk))
        for c in range(n_chunks - unroll, n_chunks - 1):
            m_run, m_chunk = step(slot, c + 1, bufs[(c + 1) % 2], c, bufs[c % 2], m_run, m_chunk)
        _, m_chunk = step(1 - slot, 0, bufs[0], n_chunks - 1, bufs[(n_chunks - 1) % 2], m_run, m_chunk)
        outs = []
        for h in range(GROUP):
            a = acc_ref[h]
            outs.append(a[:HEAD_DIM] / a[HEAD_DIM:HEAD_DIM + 1])
        rows = pl.ds(pl.multiple_of(tile * tq, tq), tq)
        o_ref[rows, :] = jnp.concatenate(outs, axis=0).T.astype(BF16)
        return m_chunk

    load_queries(0, 0)
    _, m_chunk = step(0, 0, bufs[0], None, None, None, None)
    lax.fori_loop(0, n_tiles, tile_body, m_chunk)


def _attention(qt, k, vt, tq, tk, unroll, tiles_per_step):
    bsz, _, seq_len = qt.shape
    gw = GROUP * HEAD_DIM
    tqb = tq * tiles_per_step
    assert seq_len % tqb == 0
    return pl.pallas_call(
        functools.partial(_attention_kernel, tq=tq, tk=tk, unroll=unroll),
        grid=(bsz, N_KV_HEADS, seq_len // tqb),
        in_specs=[
            pl.BlockSpec((None, gw, tqb), lambda b, g, i: (b, g, i)),
            pl.BlockSpec((None, seq_len, KV_W), lambda b, g, i: (b, 0, 0)),
            pl.BlockSpec((None, HEAD_DIM, seq_len), lambda b, g, i: (b, g, 0)),
        ],
        out_specs=pl.BlockSpec((None, tqb, gw), lambda b, g, i: (b, i, g)),
        out_shape=jax.ShapeDtypeStruct((bsz, seq_len, ATTN_W), BF16),
        scratch_shapes=[pltpu.VMEM((2, GROUP, KV_W, tq), BF16),
                        pltpu.VMEM((GROUP, tk, tq), F32),
                        pltpu.VMEM((GROUP, tk, tq), F32),
                        pltpu.VMEM((GROUP, HEAD_DIM + SUM_ROWS, tq), F32)],
        compiler_params=_params(("parallel", "parallel", "parallel")),
        name="gqa_attention",
    )(qt, k, vt)


def _mem_kv_kernel(mem_ref, w_ref, k_ref, v_ref):
    y = _dot(mem_ref[...].astype(BF16), w_ref[...])
    k_ref[...] = y[:, :D_MODEL].astype(BF16)
    v_ref[...] = y[:, D_MODEL:].astype(BF16)


def _mem_kv(mem2d, w_kv):
    rows = mem2d.shape[0]
    blk = pl.BlockSpec((MEM_TOKENS, D_MODEL), lambda i: (i, 0))
    return pl.pallas_call(
        _mem_kv_kernel,
        grid=(rows // MEM_TOKENS,),
        in_specs=[blk, _resident((D_MODEL, 2 * D_MODEL))],
        out_specs=[blk, blk],
        out_shape=[jax.ShapeDtypeStruct((rows, D_MODEL), BF16)] * 2,
        compiler_params=_params(("parallel",)),
        name="mem_kv",
    )(mem2d, w_kv)


MIX_ROWS = 512


def _mix_xattn_kernel(x_ref, o_ref, yr_ref, yi_ref, m2_ref, wg_ref, wab_ref, wfb_ref, wmix_ref,
                      ln1g_ref, ln1b_ref, wq_ref, km_ref, vm_ref, wo_ref, ln2g_ref, ln2b_ref,
                      y_ref, f_scr, *, alpha):
    n = x_ref.shape[0]
    tm = n * FAST
    m2 = m2_ref[...]
    for r in range(FAST):
        y2 = jnp.concatenate([yr_ref[r], yi_ref[r]], axis=0).astype(BF16)
        f = _dot(m2, y2)
        for gi in range(FOURIER_GROUPS):
            f_scr.at[gi][pl.ds(r, n, stride=FAST), :] = f[:, gi * FOURIER_GROUP_W:(gi + 1) * FOURIER_GROUP_W]

    pass_rows = min(MIX_ROWS, tm)

    def dense(i):
        rows = pl.ds(pl.multiple_of(i * pass_rows, pass_rows), pass_rows)
        slow = pl.ds(pl.multiple_of(i * (pass_rows // FAST), pass_rows // FAST), pass_rows // FAST)
        x = x_ref[slow].reshape(pass_rows, D_MODEL)
        gates = _dot(x.astype(BF16), wg_ref[...])
        y_attn = _dot(o_ref[rows, :], wab_ref[...])
        f = jnp.concatenate([f_scr[gi, rows, :] for gi in range(FOURIER_GROUPS)], axis=-1)
        y_four = _dot(f.astype(BF16), wfb_ref[...])
        merged = (jax.nn.sigmoid(gates[:, :D_MODEL]) * y_attn
                  + jax.nn.sigmoid(gates[:, D_MODEL:]) * y_four)
        x1 = _layernorm(alpha * x + _dot(merged.astype(BF16), wmix_ref[...]), ln1g_ref[...], ln1b_ref[...])
        q = (_dot(x1.astype(BF16), wq_ref[...]) * (MEM_HEAD_DIM ** -0.5 * LOG2E)).astype(BF16)
        heads = []
        for h in range(MEM_HEADS):
            c = h * MEM_HEAD_DIM
            s = _dot_nt(q[:, c:c + MEM_HEAD_DIM], km_ref[:, c:c + MEM_HEAD_DIM])
            p = jnp.exp2(s - jnp.max(s, axis=-1, keepdims=True))
            l = jnp.sum(p, axis=-1, keepdims=True)
            heads.append(_dot(p.astype(BF16), vm_ref[:, c:c + MEM_HEAD_DIM]) / l)
        ctx = jnp.concatenate(heads, axis=-1).astype(BF16)
        y_ref[rows, :] = _layernorm(alpha * x1 + _dot(ctx, wo_ref[...]), ln2g_ref[...], ln2b_ref[...])

    assert tm % pass_rows == 0
    if tm == pass_rows:
        dense(0)
    else:
        def body(i, carry):
            dense(i)
            return carry
        lax.fori_loop(0, tm // pass_rows, body, 0)


def _mix_xattn(x4, o2d, yr, yi, m2, wg, wab, wfb, wmix, ln1g, ln1b, wq, km, vm, wo, ln2g, ln2b, alpha):
    bsz, n, _, _ = x4.shape
    tm, tiles = n * FAST, n // FAST
    tok = lambda w: pl.BlockSpec((tm, w), lambda b, j: (b * tiles + j, 0))
    plane = pl.BlockSpec((None, FAST, n, FOURIER_W), lambda b, j: (b, j, 0, 0))
    mem = pl.BlockSpec((MEM_TOKENS, D_MODEL), lambda b, j: (b, 0))
    vec = _resident((1, D_MODEL))
    return pl.pallas_call(
        functools.partial(_mix_xattn_kernel, alpha=alpha),
        grid=(bsz, tiles),
        in_specs=[_tile_view(D_MODEL)(n), tok(ATTN_W), plane, plane, _resident((n, 2 * n)),
                  _resident((D_MODEL, 2 * D_MODEL)), _resident((ATTN_W, D_MODEL)),
                  _resident((FOURIER_W, D_MODEL)), _resident((D_MODEL, D_MODEL)), vec, vec,
                  _resident((D_MODEL, D_MODEL)), mem, mem, _resident((D_MODEL, D_MODEL)), vec, vec],
        out_specs=tok(D_MODEL),
        out_shape=jax.ShapeDtypeStruct((bsz * n * n, D_MODEL), F32),
        scratch_shapes=[pltpu.VMEM((FOURIER_GROUPS, tm, FOURIER_GROUP_W), F32)],
        compiler_params=_params(("parallel", "parallel")),
        name="mix_xattn",
    )(x4, o2d, yr, yi, m2, wg, wab, wfb, wmix, ln1g, ln1b, wq, km, vm, wo, ln2g, ln2b)


FF_CHUNK = 1024


def _mlp_kernel(x_ref, wup_ref, wdown_ref, g_ref, b_ref, y_ref, *, alpha):
    x = x_ref[...]
    xb = x.astype(BF16)
    y = alpha * x
    for c in range(0, D_FF, FF_CHUNK):
        h = jnp.maximum(_dot(xb, wup_ref[:, c:c + FF_CHUNK]), 0.0)
        y = y + _dot((h * h).astype(BF16), wdown_ref[c:c + FF_CHUNK, :])
    y_ref[...] = _layernorm(y, g_ref[...], b_ref[...]).reshape(y_ref.shape)


def _mlp(x2d, bsz, n, wup, wdown, g, b, alpha):
    tm, tiles = n * FAST, n // FAST
    vec = _resident((1, D_MODEL))
    return pl.pallas_call(
        functools.partial(_mlp_kernel, alpha=alpha),
        grid=(bsz, tiles),
        in_specs=[pl.BlockSpec((tm, D_MODEL), lambda b, j: (b * tiles + j, 0)),
                  _resident((D_MODEL, D_FF)), _resident((D_FF, D_MODEL)), vec, vec],
        out_specs=_tile_view(D_MODEL)(n),
        out_shape=jax.ShapeDtypeStruct((bsz, n, n, D_MODEL), F32),
        compiler_params=_params(("parallel", "parallel")),
        name="mlp",
    )(x2d, wup, wdown, g, b)


def _tile(n, want):
    t = min(n, want)
    assert n % t == 0
    return t


def _layer(x, mem, w, alpha):
    bsz, seq_len, _ = x.shape
    n = _grid_side(seq_len)
    x4 = x.reshape(bsz, n, n, D_MODEL)
    m1, m2 = _dft_matrices(n)
    qt, k, vt, yr, yi = _in_projection(x4, w["qkvu"], w["qk_gain"], m1)
    tq = _tile(seq_len, 512)
    o = _attention(qt, k.reshape(bsz, seq_len, KV_W), vt, tq, _tile(seq_len, 512),
                   unroll=4 if seq_len % 8192 == 0 else 2, tiles_per_step=_tile(seq_len // tq, 16))
    km, vm = _mem_kv(mem.reshape(bsz * MEM_TOKENS, D_MODEL), w["mem_kv"])
    x2 = _mix_xattn(x4, o.reshape(bsz * seq_len, ATTN_W), yr, yi, m2, w["gate"], w["attn_branch"],
                    w["fourier_branch"], w["mix_out"], w["ln1_g"], w["ln1_b"], w["mem_q"], km, vm,
                    w["mem_o"], w["ln2_g"], w["ln2_b"], alpha)
    y = _mlp(x2, bsz, n, w["up"], w["down"], w["ln3_g"], w["ln3_b"], alpha)
    return y.reshape(bsz, seq_len, D_MODEL)


def kernel(x_prompt, x_sample, mem_prompt, mem_sample, w_in, q_norm, k_norm, w_attn_branch, w_fourier_branch, w_mix_out, ln1_g, ln1_b, w_mem_q, w_mem_k, w_mem_v, w_mem_o, ln2_g, ln2_b, w_up, w_down, ln3_g, ln3_b):
    depth = w_in.shape[0]
    alpha = float((2 * depth) ** 0.25)
    y_prompt, y_sample = x_prompt, x_sample
    for l in range(depth):
        row = lambda v: v[l].reshape(1, -1).astype(F32)
        w = {
            "qkvu": w_in[l, :, :QKVU_W].astype(BF16),
            "gate": w_in[l, :, QKVU_W:].astype(BF16),
            "qk_gain": jnp.concatenate([jnp.tile(q_norm[l], N_HEADS), jnp.tile(k_norm[l], N_KV_HEADS)]).reshape(1, -1).astype(F32),
            "attn_branch": w_attn_branch[l].astype(BF16),
            "fourier_branch": w_fourier_branch[l].astype(BF16),
            "mix_out": w_mix_out[l].astype(BF16),
            "ln1_g": row(ln1_g), "ln1_b": row(ln1_b),
            "mem_q": w_mem_q[l].astype(BF16),
            "mem_kv": jnp.concatenate([w_mem_k[l], w_mem_v[l]], axis=1).astype(BF16),
            "mem_o": w_mem_o[l].astype(BF16),
            "ln2_g": row(ln2_g), "ln2_b": row(ln2_b),
            "up": w_up[l].astype(BF16), "down": w_down[l].astype(BF16),
            "ln3_g": row(ln3_g), "ln3_b": row(ln3_b),
        }
        y_prompt = _layer(y_prompt, mem_prompt, w, alpha)
        y_sample = _layer(y_sample, mem_sample, w, alpha)
    return (y_prompt, y_sample)
```

```python
import functools
import math

import numpy as np
import jax
import jax.numpy as jnp
from jax import lax
from jax.experimental import pallas as pl
from jax.experimental.pallas import tpu as pltpu

D_MODEL = 1024
GRID_W = 64
N_HEADS = 8
N_KV_HEADS = 2
HEAD_DIM = 64
GROUP = N_HEADS // N_KV_HEADS
ATTN_W = N_HEADS * HEAD_DIM
KV_W = N_KV_HEADS * HEAD_DIM
ROPE_THETA = 10000.0
AXIS_ROT = HEAD_DIM // 2
FOURIER_GROUPS = 4
FOURIER_GROUP_W = 128
FOURIER_W = FOURIER_GROUPS * FOURIER_GROUP_W
QKVU_W = ATTN_W + 2 * KV_W + FOURIER_W
MEM_TOKENS = 256
MEM_HEADS = 4
MEM_HEAD_DIM = D_MODEL // MEM_HEADS
D_FF = 4 * D_MODEL
RMS_EPS = 1e-6
LN_EPS = 1e-5
LOG2E = math.log2(math.e)

V7X_VMEM_BYTES = 64 * 1024 * 1024
VMEM_LIMIT_BYTES = V7X_VMEM_BYTES - 12 * 1024 * 1024
LANES = 128

BF16 = jnp.bfloat16
F32 = jnp.float32


def _dot(a, b):
    return jnp.dot(a, b, preferred_element_type=F32)


def _dot_nt(a, b):
    return lax.dot_general(a, b, (((1,), (1,)), ((), ())), preferred_element_type=F32)


def _resident(shape):
    n = len(shape)
    return pl.BlockSpec(shape, lambda *_: (0,) * n, pipeline_mode=pl.Buffered(1))


def _params(semantics):
    return pltpu.CompilerParams(dimension_semantics=semantics, vmem_limit_bytes=VMEM_LIMIT_BYTES)


def _layernorm(x, g, b):
    mu = jnp.mean(x, axis=-1, keepdims=True)
    xc = x - mu
    var = jnp.mean(xc * xc, axis=-1, keepdims=True)
    return xc * lax.rsqrt(var + LN_EPS) * g + b


FAST = 8


def _grid_side(seq_len):
    n = math.isqrt(seq_len)
    assert n * n == seq_len and n % FAST == 0, "sequence length must be a square of a multiple of 8"
    return n


def _tile_view(width):
    return lambda n: pl.BlockSpec((None, n, FAST, width), lambda b, j: (b, 0, j, 0))


def _inproj_kernel(x_ref, w_ref, gain_ref, cos_ref, sin_ref, seg_ref, cdft_ref, m1_ref, twc_ref, tws_ref,
                   qt_ref, k_ref, vt_ref, yr_ref, yi_ref, zc_scr, zs_scr):
    n = x_ref.shape[0]
    tm = n * FAST
    xb = x_ref[...].reshape(tm, D_MODEL).astype(BF16)
    h = _dot(xb, w_ref[...])
    qk_w = ATTN_W + KV_W
    qk = h[:, :qk_w]
    sq = (qk * qk).astype(BF16)
    seg = seg_ref[...]
    ms = jnp.concatenate(
        [_dot(sq[:, c:c + 2 * LANES], seg) for c in range(0, ATTN_W, 2 * LANES)]
        + [_dot(sq[:, ATTN_W:qk_w], seg[:KV_W, :KV_W])], axis=-1)
    qkn = qk * lax.rsqrt(ms + RMS_EPS) * gain_ref[...]
    cos = cos_ref[...].reshape(tm, LANES)
    sin = sin_ref[...].reshape(tm, LANES)
    lane = lax.broadcasted_iota(jnp.int32, cos.shape, 1)
    first_half = (lane % HEAD_DIM) < (HEAD_DIM // 2)
    slabs = []
    for c in range(0, qk_w, LANES):
        xs = qkn[:, c:c + LANES]
        ahead = pltpu.roll(xs, LANES - HEAD_DIM // 2, 1)
        behind = pltpu.roll(xs, HEAD_DIM // 2, 1)
        slabs.append(xs * cos + jnp.where(first_half, ahead, behind) * sin)
    q = jnp.concatenate(slabs[:ATTN_W // LANES], axis=-1) * (HEAD_DIM ** -0.5 * LOG2E)
    qt_ref[...] = q.T.astype(BF16)
    k_ref[...] = slabs[-1].astype(BF16)
    vt_ref[...] = h[:, qk_w:qk_w + KV_W].T.astype(BF16)
    ub = h[:, qk_w + KV_W:].astype(BF16)
    cdft = cdft_ref[...]
    for gi in range(FOURIER_GROUPS):
        c = gi * FOURIER_GROUP_W
        z = _dot(ub[:, c:c + FOURIER_GROUP_W], cdft)
        zc_scr[gi] = z[:, :FOURIER_GROUP_W]
        zs_scr[gi] = z[:, FOURIER_GROUP_W:]
    m1 = m1_ref[...]
    reps = FOURIER_W // LANES
    for r in range(FAST):
        rows = pl.ds(r, n, stride=FAST)
        gather = lambda scr: jnp.concatenate([scr.at[gi][rows, :] for gi in range(FOURIER_GROUPS)], axis=-1)
        x2 = jnp.concatenate([gather(zc_scr), gather(zs_scr)], axis=0).astype(BF16)
        y = _dot(m1, x2)
        yr, yi = y[:n], y[n:]
        c = jnp.concatenate([twc_ref[r]] * reps, axis=-1)
        s = jnp.concatenate([tws_ref[r]] * reps, axis=-1)
        yr_ref[:, r, :] = yr * c + yi * s
        yi_ref[:, r, :] = yi * c - yr * s


def _rope_tables(seq_len):
    rows = seq_len // GRID_W
    row = jnp.repeat(jnp.arange(rows, dtype=F32), GRID_W)
    col = jnp.tile(jnp.arange(GRID_W, dtype=F32), rows)
    freqs = ROPE_THETA ** (-jnp.arange(0, AXIS_ROT, 2, dtype=F32) / AXIS_ROT)
    ang = jnp.concatenate([row[:, None] * freqs, col[:, None] * freqs], axis=-1)
    ang = jnp.concatenate([ang, ang], axis=-1)
    cos, sin = jnp.cos(ang), jnp.sin(ang)
    sign = jnp.where(jnp.arange(HEAD_DIM) < HEAD_DIM // 2, -1.0, 1.0).astype(F32)
    reps = LANES // HEAD_DIM
    return jnp.tile(cos, (1, reps)), jnp.tile(sin * sign, (1, reps))


def _segment_mean_matrix():
    idx = np.arange(2 * LANES)
    m = (idx[:, None] // HEAD_DIM == idx[None, :] // HEAD_DIM).astype(np.float32) / HEAD_DIM
    return jnp.asarray(m, dtype=BF16)


def _channel_dft_matrix():
    n = FOURIER_GROUP_W
    jk = np.outer(np.arange(n), np.arange(n)) % n
    ang = 2.0 * np.pi * jk / n
    m = np.concatenate([np.cos(ang), np.sin(ang)], axis=1) / math.sqrt(n)
    return jnp.asarray(m, dtype=BF16)


def _dft_matrices(n):
    ang = 2.0 * np.pi * (np.outer(np.arange(n), np.arange(n)) % n) / n
    c, s = np.cos(ang) / math.sqrt(n), np.sin(ang) / math.sqrt(n)
    m1 = np.block([[c, -s], [-s, -c]])
    m2 = np.concatenate([c, s], axis=1)
    return jnp.asarray(m1, dtype=BF16), jnp.asarray(m2, dtype=BF16)


def _twiddles(n):
    seq_len = n * n
    idx = jnp.arange(n, dtype=jnp.int32)
    ang = ((idx[:, None] * idx[None, :]) % seq_len).astype(F32) * (2.0 * math.pi / seq_len)
    rep = lambda t: jnp.broadcast_to(t[:, :, None], (n, n, LANES))
    return rep(jnp.cos(ang)), rep(jnp.sin(ang))


def _in_projection(x4, w_qkvu, gain, m1):
    bsz, n, _, _ = x4.shape
    seq_len, tm, tiles = n * n, n * FAST, n // FAST
    cos, sin = _rope_tables(seq_len)
    twc, tws = _twiddles(n)
    pos = pl.BlockSpec((n, FAST, LANES), lambda b, j: (0, j, 0))
    tw = pl.BlockSpec((FAST, n, LANES), lambda b, j: (j, 0, 0))
    tr = lambda rows: pl.BlockSpec((None, rows, tm), lambda b, j: (b, 0, j))
    plane = _tile_view(FOURIER_W)(n)
    return pl.pallas_call(
        _inproj_kernel,
        grid=(bsz, tiles),
        in_specs=[
            _tile_view(D_MODEL)(n),
            _resident((D_MODEL, QKVU_W)),
            _resident((1, ATTN_W + KV_W)),
            pos, pos,
            _resident((2 * LANES, 2 * LANES)),
            _resident((FOURIER_GROUP_W, 2 * FOURIER_GROUP_W)),
            _resident((2 * n, 2 * n)),
            tw, tw,
        ],
        out_specs=[
            tr(ATTN_W),
            pl.BlockSpec((tm, KV_W), lambda b, j: (b * tiles + j, 0)),
            tr(KV_W),
            plane, plane,
        ],
        out_shape=[
            jax.ShapeDtypeStruct((bsz, ATTN_W, seq_len), BF16),
            jax.ShapeDtypeStruct((bsz * seq_len, KV_W), BF16),
            jax.ShapeDtypeStruct((bsz, KV_W, seq_len), BF16),
            jax.ShapeDtypeStruct((bsz, n, n, FOURIER_W), F32),
            jax.ShapeDtypeStruct((bsz, n, n, FOURIER_W), F32),
        ],
        scratch_shapes=[pltpu.VMEM((FOURIER_GROUPS, tm, FOURIER_GROUP_W), F32)] * 2,
        compiler_params=_params(("parallel", "parallel")),
        name="in_projection",
    )(x4, w_qkvu, gain, cos.reshape(n, n, LANES), sin.reshape(n, n, LANES),
      _segment_mean_matrix(), _channel_dft_matrix(), m1, twc, tws)


SUM_ROWS = 16
KEY_SUBTILE = 256
SAFE_SCORE_BOUND = 60.0
BOUND_SLACK = 1.05
BOUNDED_GROUP_KEYS = 4096


def _attention_kernel(bound_ref, qt_ref, k_ref, vt_ref, o_ref, qpad_ref, sa_ref, sb_ref, acc_ref, *,
                      tq, tk, unroll):
    n_tiles = qt_ref.shape[-1] // tq
    seq_len = k_ref.shape[0]
    n_chunks = seq_len // tk
    assert unroll % 2 == 0 and n_chunks % unroll == 0 and tk % KEY_SUBTILE == 0
    kv_head = pl.program_id(1)
    row_head = lax.broadcasted_iota(jnp.int32, (KV_W, tq), 0) // HEAD_DIM
    ones = jnp.ones((SUM_ROWS, tk), BF16)
    bufs = (sa_ref, sb_ref)

    def load_queries(tile, slot):
        cols = pl.ds(pl.multiple_of(tile * tq, tq), tq)
        for h in range(GROUP):
            qh = qt_ref[h * HEAD_DIM:(h + 1) * HEAD_DIM, cols]
            q_rep = jnp.concatenate([qh] * N_KV_HEADS, axis=0)
            qpad_ref[slot, h] = jnp.where(row_head == kv_head, q_rep, jnp.zeros_like(q_rep))

    def key_chunk(c):
        return pl.ds(pl.multiple_of(c * tk, tk), tk)

    def step(q_slot, c_next, s_next_ref, c_cur, s_cur_ref, m_run, m_chunk):
        kc = k_ref[key_chunk(c_next), :]
        if c_cur is not None:
            vc = jnp.concatenate([vt_ref[:, key_chunk(c_cur)], ones], axis=0)
        m_next, maxima = [], []
        for h in range(GROUP):
            if c_cur is not None:
                m_new = jnp.maximum(m_run[h], m_chunk[h])
                alpha = jnp.exp2(m_run[h] - m_new)
                m_next.append(m_new)
            pv, mx = None, None
            for t in range(0, tk, KEY_SUBTILE):
                s = _dot(kc[t:t + KEY_SUBTILE], qpad_ref[q_slot, h])
                s_next_ref[h, t:t + KEY_SUBTILE, :] = s
                smax = jnp.max(s, axis=0, keepdims=True)
                mx = smax if mx is None else jnp.maximum(mx, smax)
                if c_cur is not None:
                    p = jnp.exp2(s_cur_ref[h, t:t + KEY_SUBTILE, :] - m_new).astype(BF16)
                    d = _dot(vc[:, t:t + KEY_SUBTILE], p)
                    pv = d if pv is None else pv + d
            if c_cur is not None:
                acc_ref[h] = alpha * acc_ref[h] + pv
            maxima.append(mx)
        return tuple(m_next), tuple(maxima)

    def tile_body(tile, m_chunk):
        slot = tile % 2
        load_queries(jnp.minimum(tile + 1, n_tiles - 1), 1 - slot)
        acc_ref[...] = jnp.zeros_like(acc_ref)
        m_run = tuple(jnp.full((1, tq), -jnp.inf, F32) for _ in range(GROUP))

        def group(j, carry):
            m_run, m_chunk = carry
            for u in range(unroll):
                c = unroll * j + u
                m_run, m_chunk = step(slot, c + 1, bufs[(u + 1) % 2], c, bufs[u % 2], m_run, m_chunk)
            return m_run, m_chunk

        m_run, m_chunk = lax.fori_loop(0, n_chunks // unroll - 1, group, (m_run, m_chunk))
        for c in range(n_chunks - unroll, n_chunks - 1):
            m_run, m_chunk = step(slot, c + 1, bufs[(c + 1) % 2], c, bufs[c % 2], m_run, m_chunk)
        _, m_chunk = step(1 - slot, 0, bufs[0], n_chunks - 1, bufs[(n_chunks - 1) % 2], m_run, m_chunk)
        write_tile(tile)
        return m_chunk

    def write_tile(tile):
        outs = []
        for h in range(GROUP):
            a = acc_ref[h]
            outs.append(a[:HEAD_DIM] / a[HEAD_DIM:HEAD_DIM + 1])
        rows = pl.ds(pl.multiple_of(tile * tq, tq), tq)
        o_ref[rows, :] = jnp.concatenate(outs, axis=0).T.astype(BF16)

    group_keys = min(seq_len, BOUNDED_GROUP_KEYS)
    n_sub = group_keys // KEY_SUBTILE
    sub_ones = jnp.ones((SUM_ROWS, KEY_SUBTILE), BF16)

    def bounded_tile(tile, carry):
        load_queries(tile, 0)
        acc_ref[...] = jnp.zeros_like(acc_ref)

        def group(j, carry):
            def keys(t):
                return pl.ds(pl.multiple_of(j * group_keys + t * KEY_SUBTILE, KEY_SUBTILE), KEY_SUBTILE)
            pv = [None] * GROUP
            s_prev = [None] * GROUP
            for t in range(n_sub + 1):
                if t < n_sub:
                    kt = k_ref[keys(t), :]
                if t > 0:
                    vt = jnp.concatenate([vt_ref[:, keys(t - 1)], sub_ones], axis=0)
                for h in range(GROUP):
                    s_cur = _dot(kt, qpad_ref[0, h]) if t < n_sub else None
                    if t > 0:
                        d = _dot(vt, jnp.exp2(s_prev[h]).astype(BF16))
                        pv[h] = d if pv[h] is None else pv[h] + d
                    s_prev[h] = s_cur
            for h in range(GROUP):
                acc_ref[h] += pv[h]
            return carry

        lax.fori_loop(0, seq_len // group_keys, group, 0)
        write_tile(tile)
        return carry

    bounded = bound_ref[0] <= SAFE_SCORE_BOUND

    @pl.when(bounded)
    def _():
        lax.fori_loop(0, n_tiles, bounded_tile, 0)

    @pl.when(jnp.logical_not(bounded))
    def _():
        load_queries(0, 0)
        _, m_chunk = step(0, 0, bufs[0], None, None, None, None)
        lax.fori_loop(0, n_tiles, tile_body, m_chunk)


def _attention(score_bound, qt, k, vt, tq, tk, unroll, tiles_per_step):
    bsz, _, seq_len = qt.shape
    gw = GROUP * HEAD_DIM
    tqb = tq * tiles_per_step
    assert seq_len % tqb == 0
    return pl.pallas_call(
        functools.partial(_attention_kernel, tq=tq, tk=tk, unroll=unroll),
        grid=(bsz, N_KV_HEADS, seq_len // tqb),
        in_specs=[
            pl.BlockSpec(memory_space=pltpu.SMEM),
            pl.BlockSpec((None, gw, tqb), lambda b, g, i: (b, g, i)),
            pl.BlockSpec((None, seq_len, KV_W), lambda b, g, i: (b, 0, 0)),
            pl.BlockSpec((None, HEAD_DIM, seq_len), lambda b, g, i: (b, g, 0)),
        ],
        out_specs=pl.BlockSpec((None, tqb, gw), lambda b, g, i: (b, i, g)),
        out_shape=jax.ShapeDtypeStruct((bsz, seq_len, ATTN_W), BF16),
        scratch_shapes=[pltpu.VMEM((2, GROUP, KV_W, tq), BF16),
                        pltpu.VMEM((GROUP, tk, tq), F32),
                        pltpu.VMEM((GROUP, tk, tq), F32),
                        pltpu.VMEM((GROUP, HEAD_DIM + SUM_ROWS, tq), F32)],
        compiler_params=_params(("parallel", "parallel", "parallel")),
        name="gqa_attention",
    )(score_bound, qt, k, vt)


def _mem_kv_kernel(mem_ref, w_ref, k_ref, v_ref):
    y = _dot(mem_ref[...].astype(BF16), w_ref[...])
    k_ref[...] = y[:, :D_MODEL].astype(BF16)
    v_ref[...] = y[:, D_MODEL:].astype(BF16)


def _mem_kv(mem2d, w_kv):
    rows = mem2d.shape[0]
    blk = pl.BlockSpec((MEM_TOKENS, D_MODEL), lambda i: (i, 0))
    return pl.pallas_call(
        _mem_kv_kernel,
        grid=(rows // MEM_TOKENS,),
        in_specs=[blk, _resident((D_MODEL, 2 * D_MODEL))],
        out_specs=[blk, blk],
        out_shape=[jax.ShapeDtypeStruct((rows, D_MODEL), BF16)] * 2,
        compiler_params=_params(("parallel",)),
        name="mem_kv",
    )(mem2d, w_kv)


MIX_ROWS = 512


def _mix_xattn_kernel(x_ref, o_ref, yr_ref, yi_ref, m2_ref, wg_ref, wab_ref, wfb_ref, wmix_ref,
                      ln1g_ref, ln1b_ref, wq_ref, km_ref, vm_ref, wo_ref, ln2g_ref, ln2b_ref,
                      y_ref, f_scr, *, alpha):
    n = x_ref.shape[0]
    tm = n * FAST
    m2 = m2_ref[...]
    for r in range(FAST):
        y2 = jnp.concatenate([yr_ref[r], yi_ref[r]], axis=0).astype(BF16)
        f = _dot(m2, y2)
        for gi in range(FOURIER_GROUPS):
            f_scr.at[gi][pl.ds(r, n, stride=FAST), :] = f[:, gi * FOURIER_GROUP_W:(gi + 1) * FOURIER_GROUP_W]

    pass_rows = min(MIX_ROWS, tm)

    def dense(i):
        rows = pl.ds(pl.multiple_of(i * pass_rows, pass_rows), pass_rows)
        slow = pl.ds(pl.multiple_of(i * (pass_rows // FAST), pass_rows // FAST), pass_rows // FAST)
        x = x_ref[slow].reshape(pass_rows, D_MODEL)
        gates = _dot(x.astype(BF16), wg_ref[...])
        y_attn = _dot(o_ref[rows, :], wab_ref[...])
        f = jnp.concatenate([f_scr[gi, rows, :] for gi in range(FOURIER_GROUPS)], axis=-1)
        y_four = _dot(f.astype(BF16), wfb_ref[...])
        merged = (jax.nn.sigmoid(gates[:, :D_MODEL]) * y_attn
                  + jax.nn.sigmoid(gates[:, D_MODEL:]) * y_four)
        x1 = _layernorm(alpha * x + _dot(merged.astype(BF16), wmix_ref[...]), ln1g_ref[...], ln1b_ref[...])
        q = (_dot(x1.astype(BF16), wq_ref[...]) * (MEM_HEAD_DIM ** -0.5 * LOG2E)).astype(BF16)
        heads = []
        for h in range(MEM_HEADS):
            c = h * MEM_HEAD_DIM
            s = _dot_nt(q[:, c:c + MEM_HEAD_DIM], km_ref[:, c:c + MEM_HEAD_DIM])
            p = jnp.exp2(s - jnp.max(s, axis=-1, keepdims=True))
            l = jnp.sum(p, axis=-1, keepdims=True)
            heads.append(_dot(p.astype(BF16), vm_ref[:, c:c + MEM_HEAD_DIM]) / l)
        ctx = jnp.concatenate(heads, axis=-1).astype(BF16)
        y_ref[rows, :] = _layernorm(alpha * x1 + _dot(ctx, wo_ref[...]), ln2g_ref[...], ln2b_ref[...])

    assert tm % pass_rows == 0
    if tm == pass_rows:
        dense(0)
    else:
        def body(i, carry):
            dense(i)
            return carry
        lax.fori_loop(0, tm // pass_rows, body, 0)


def _mix_xattn(x4, o2d, yr, yi, m2, wg, wab, wfb, wmix, ln1g, ln1b, wq, km, vm, wo, ln2g, ln2b, alpha):
    bsz, n, _, _ = x4.shape
    tm, tiles = n * FAST, n // FAST
    tok = lambda w: pl.BlockSpec((tm, w), lambda b, j: (b * tiles + j, 0))
    plane = pl.BlockSpec((None, FAST, n, FOURIER_W), lambda b, j: (b, j, 0, 0))
    mem = pl.BlockSpec((MEM_TOKENS, D_MODEL), lambda b, j: (b, 0))
    vec = _resident((1, D_MODEL))
    return pl.pallas_call(
        functools.partial(_mix_xattn_kernel, alpha=alpha),
        grid=(bsz, tiles),
        in_specs=[_tile_view(D_MODEL)(n), tok(ATTN_W), plane, plane, _resident((n, 2 * n)),
                  _resident((D_MODEL, 2 * D_MODEL)), _resident((ATTN_W, D_MODEL)),
                  _resident((FOURIER_W, D_MODEL)), _resident((D_MODEL, D_MODEL)), vec, vec,
                  _resident((D_MODEL, D_MODEL)), mem, mem, _resident((D_MODEL, D_MODEL)), vec, vec],
        out_specs=tok(D_MODEL),
        out_shape=jax.ShapeDtypeStruct((bsz * n * n, D_MODEL), F32),
        scratch_shapes=[pltpu.VMEM((FOURIER_GROUPS, tm, FOURIER_GROUP_W), F32)],
        compiler_params=_params(("parallel", "parallel")),
        name="mix_xattn",
    )(x4, o2d, yr, yi, m2, wg, wab, wfb, wmix, ln1g, ln1b, wq, km, vm, wo, ln2g, ln2b)


FF_CHUNK = 1024


def _mlp_kernel(x_ref, wup_ref, wdown_ref, g_ref, b_ref, y_ref, *, alpha):
    x = x_ref[...]
    xb = x.astype(BF16)
    y = alpha * x
    for c in range(0, D_FF, FF_CHUNK):
        h = jnp.maximum(_dot(xb, wup_ref[:, c:c + FF_CHUNK]), 0.0)
        y = y + _dot((h * h).astype(BF16), wdown_ref[c:c + FF_CHUNK, :])
    y_ref[...] = _layernorm(y, g_ref[...], b_ref[...]).reshape(y_ref.shape)


def _mlp(x2d, bsz, n, wup, wdown, g, b, alpha):
    tm, tiles = n * FAST, n // FAST
    vec = _resident((1, D_MODEL))
    return pl.pallas_call(
        functools.partial(_mlp_kernel, alpha=alpha),
        grid=(bsz, tiles),
        in_specs=[pl.BlockSpec((tm, D_MODEL), lambda b, j: (b * tiles + j, 0)),
                  _resident((D_MODEL, D_FF)), _resident((D_FF, D_MODEL)), vec, vec],
        out_specs=_tile_view(D_MODEL)(n),
        out_shape=jax.ShapeDtypeStruct((bsz, n, n, D_MODEL), F32),
        compiler_params=_params(("parallel", "parallel")),
        name="mlp",
    )(x2d, wup, wdown, g, b)


def _tile(n, want):
    t = min(n, want)
    assert n % t == 0
    return t


def _layer(x, mem, w, alpha):
    bsz, seq_len, _ = x.shape
    n = _grid_side(seq_len)
    x4 = x.reshape(bsz, n, n, D_MODEL)
    m1, m2 = _dft_matrices(n)
    qt, k, vt, yr, yi = _in_projection(x4, w["qkvu"], w["qk_gain"], m1)
    tq = _tile(seq_len, 256)
    o = _attention(w["score_bound"], qt, k.reshape(bsz, seq_len, KV_W), vt, tq, _tile(seq_len, 512),
                   unroll=4 if seq_len % 8192 == 0 else 2, tiles_per_step=_tile(seq_len // tq, 16))
    km, vm = _mem_kv(mem.reshape(bsz * MEM_TOKENS, D_MODEL), w["mem_kv"])
    x2 = _mix_xattn(x4, o.reshape(bsz * seq_len, ATTN_W), yr, yi, m2, w["gate"], w["attn_branch"],
                    w["fourier_branch"], w["mix_out"], w["ln1_g"], w["ln1_b"], w["mem_q"], km, vm,
                    w["mem_o"], w["ln2_g"], w["ln2_b"], alpha)
    y = _mlp(x2, bsz, n, w["up"], w["down"], w["ln3_g"], w["ln3_b"], alpha)
    return y.reshape(bsz, seq_len, D_MODEL)


def kernel(x_prompt, x_sample, mem_prompt, mem_sample, w_in, q_norm, k_norm, w_attn_branch, w_fourier_branch, w_mix_out, ln1_g, ln1_b, w_mem_q, w_mem_k, w_mem_v, w_mem_o, ln2_g, ln2_b, w_up, w_down, ln3_g, ln3_b):
    depth = w_in.shape[0]
    alpha = float((2 * depth) ** 0.25)
    y_prompt, y_sample = x_prompt, x_sample
    for l in range(depth):
        row = lambda v: v[l].reshape(1, -1).astype(F32)
        w = {
            "qkvu": w_in[l, :, :QKVU_W].astype(BF16),
            "gate": w_in[l, :, QKVU_W:].astype(BF16),
            "qk_gain": jnp.concatenate([jnp.tile(q_norm[l], N_HEADS), jnp.tile(k_norm[l], N_KV_HEADS)]).reshape(1, -1).astype(F32),
            "score_bound": (HEAD_DIM ** 0.5 * LOG2E * BOUND_SLACK * jnp.max(jnp.abs(q_norm[l]))
                            * jnp.max(jnp.abs(k_norm[l]))).reshape(1).astype(F32),
            "attn_branch": w_attn_branch[l].astype(BF16),
            "fourier_branch": w_fourier_branch[l].astype(BF16),
            "mix_out": w_mix_out[l].astype(BF16),
            "ln1_g": row(ln1_g), "ln1_b": row(ln1_b),
            "mem_q": w_mem_q[l].astype(BF16),
            "mem_kv": jnp.concatenate([w_mem_k[l], w_mem_v[l]], axis=1).astype(BF16),
            "mem_o": w_mem_o[l].astype(BF16),
            "ln2_g": row(ln2_g), "ln2_b": row(ln2_b),
            "up": w_up[l].astype(BF16), "down": w_down[l].astype(BF16),
            "ln3_g": row(ln3_g), "ln3_b": row(ln3_b),
        }
        y_prompt = _layer(y_prompt, mem_prompt, w, alpha)
        y_sample = _layer(y_sample, mem_sample, w, alpha)
    return (y_prompt, y_sample)
```

```python
import functools
import math

import numpy as np
import jax
import jax.numpy as jnp
from jax import lax
from jax.experimental import pallas as pl
from jax.experimental.pallas import tpu as pltpu

D_MODEL = 1024
GRID_W = 64
N_HEADS = 8
N_KV_HEADS = 2
HEAD_DIM = 64
GROUP = N_HEADS // N_KV_HEADS
ATTN_W = N_HEADS * HEAD_DIM
KV_W = N_KV_HEADS * HEAD_DIM
ROPE_THETA = 10000.0
AXIS_ROT = HEAD_DIM // 2
FOURIER_GROUPS = 4
FOURIER_GROUP_W = 128
FOURIER_W = FOURIER_GROUPS * FOURIER_GROUP_W
QKVU_W = ATTN_W + 2 * KV_W + FOURIER_W
MEM_TOKENS = 256
MEM_HEADS = 4
MEM_HEAD_DIM = D_MODEL // MEM_HEADS
D_FF = 4 * D_MODEL
RMS_EPS = 1e-6
LN_EPS = 1e-5
LOG2E = math.log2(math.e)

V7X_VMEM_BYTES = 64 * 1024 * 1024
VMEM_LIMIT_BYTES = V7X_VMEM_BYTES - 12 * 1024 * 1024
LANES = 128
SUBLANES = 8

BF16 = jnp.bfloat16
F32 = jnp.float32


def _dot(a, b):
    return jnp.dot(a, b, preferred_element_type=F32)


def _dot_nt(a, b):
    return lax.dot_general(a, b, (((1,), (1,)), ((), ())), preferred_element_type=F32)


def _resident(shape):
    n = len(shape)
    return pl.BlockSpec(shape, lambda *_: (0,) * n, pipeline_mode=pl.Buffered(1))


def _params(semantics):
    return pltpu.CompilerParams(dimension_semantics=semantics, vmem_limit_bytes=VMEM_LIMIT_BYTES)


def _layernorm(x, g, b):
    mu = jnp.mean(x, axis=-1, keepdims=True)
    xc = x - mu
    var = jnp.mean(xc * xc, axis=-1, keepdims=True)
    return xc * lax.rsqrt(var + LN_EPS) * g + b


FAST = 8


def _grid_side(seq_len):
    n = math.isqrt(seq_len)
    assert n * n == seq_len and n % FAST == 0, "sequence length must be a square of a multiple of 8"
    return n


def _tile_view(width):
    return lambda n: pl.BlockSpec((None, n, FAST, width), lambda b, j: (b, 0, j, 0))


def _inproj_kernel(x_ref, w_ref, gain_ref, cos_ref, sin_ref, seg_ref, cdft_ref, m1_ref, twc_ref, tws_ref,
                   qt_ref, k_ref, vt_ref, yr_ref, yi_ref, zc_scr, zs_scr, yr_scr, yi_scr):
    n = x_ref.shape[0]
    tm = n * FAST
    xb = x_ref[...].reshape(tm, D_MODEL).astype(BF16)
    h = _dot(xb, w_ref[...])
    qk_w = ATTN_W + KV_W
    qk = h[:, :qk_w]
    sq = (qk * qk).astype(BF16)
    seg = seg_ref[...]
    ms = jnp.concatenate(
        [_dot(sq[:, c:c + 2 * LANES], seg) for c in range(0, ATTN_W, 2 * LANES)]
        + [_dot(sq[:, ATTN_W:qk_w], seg[:KV_W, :KV_W])], axis=-1)
    qkn = qk * lax.rsqrt(ms + RMS_EPS) * gain_ref[...]
    cos = cos_ref[...].reshape(tm, LANES)
    sin = sin_ref[...].reshape(tm, LANES)
    lane = lax.broadcasted_iota(jnp.int32, cos.shape, 1)
    first_half = (lane % HEAD_DIM) < (HEAD_DIM // 2)
    slabs = []
    for c in range(0, qk_w, LANES):
        xs = qkn[:, c:c + LANES]
        ahead = pltpu.roll(xs, LANES - HEAD_DIM // 2, 1)
        behind = pltpu.roll(xs, HEAD_DIM // 2, 1)
        slabs.append(xs * cos + jnp.where(first_half, ahead, behind) * sin)
    q = jnp.concatenate(slabs[:ATTN_W // LANES], axis=-1) * (HEAD_DIM ** -0.5 * LOG2E)
    qt_ref[...] = q.T.astype(BF16)
    k_ref[...] = slabs[-1].astype(BF16)
    vt_ref[...] = h[:, qk_w:qk_w + KV_W].T.astype(BF16)
    ub = h[:, qk_w + KV_W:].astype(BF16)
    cdft = cdft_ref[...]
    for gi in range(FOURIER_GROUPS):
        c = gi * FOURIER_GROUP_W
        z = _dot(ub[:, c:c + FOURIER_GROUP_W], cdft)
        zc_scr[gi] = z[:, :FOURIER_GROUP_W]
        zs_scr[gi] = z[:, FOURIER_GROUP_W:]
    m1 = m1_ref[...]
    reps = FOURIER_W // LANES
    for r in range(FAST):
        rows = pl.ds(r, n, stride=FAST)
        gather = lambda scr: jnp.concatenate([scr.at[gi][rows, :] for gi in range(FOURIER_GROUPS)], axis=-1)
        x2 = jnp.concatenate([gather(zc_scr), gather(zs_scr)], axis=0).astype(BF16)
        y = _dot(m1, x2)
        yr, yi = y[:n], y[n:]
        c = jnp.concatenate([twc_ref[r]] * reps, axis=-1)
        s = jnp.concatenate([tws_ref[r]] * reps, axis=-1)
        for scr, val in ((yr_scr, yr * c + yi * s), (yi_scr, yi * c - yr * s)):
            for gi in range(FOURIER_GROUPS):
                scr.at[gi][rows, :] = val[:, gi * FOURIER_GROUP_W:(gi + 1) * FOURIER_GROUP_W]
    for gi in range(FOURIER_GROUPS):
        yr_ref[gi] = yr_scr[gi].reshape(n, FAST, FOURIER_GROUP_W)
        yi_ref[gi] = yi_scr[gi].reshape(n, FAST, FOURIER_GROUP_W)


def _rope_tables(seq_len):
    rows = seq_len // GRID_W
    row = jnp.repeat(jnp.arange(rows, dtype=F32), GRID_W)
    col = jnp.tile(jnp.arange(GRID_W, dtype=F32), rows)
    freqs = ROPE_THETA ** (-jnp.arange(0, AXIS_ROT, 2, dtype=F32) / AXIS_ROT)
    ang = jnp.concatenate([row[:, None] * freqs, col[:, None] * freqs], axis=-1)
    ang = jnp.concatenate([ang, ang], axis=-1)
    cos, sin = jnp.cos(ang), jnp.sin(ang)
    sign = jnp.where(jnp.arange(HEAD_DIM) < HEAD_DIM // 2, -1.0, 1.0).astype(F32)
    reps = LANES // HEAD_DIM
    return jnp.tile(cos, (1, reps)), jnp.tile(sin * sign, (1, reps))


def _segment_mean_matrix():
    idx = np.arange(2 * LANES)
    m = (idx[:, None] // HEAD_DIM == idx[None, :] // HEAD_DIM).astype(np.float32) / HEAD_DIM
    return jnp.asarray(m, dtype=BF16)


def _channel_dft_matrix():
    n = FOURIER_GROUP_W
    jk = np.outer(np.arange(n), np.arange(n)) % n
    ang = 2.0 * np.pi * jk / n
    m = np.concatenate([np.cos(ang), np.sin(ang)], axis=1) / math.sqrt(n)
    return jnp.asarray(m, dtype=BF16)


def _dft_matrices(n):
    ang = 2.0 * np.pi * (np.outer(np.arange(n), np.arange(n)) % n) / n
    c, s = np.cos(ang) / math.sqrt(n), np.sin(ang) / math.sqrt(n)
    m1 = np.block([[c, -s], [-s, -c]])
    m2 = np.concatenate([c, s], axis=1)
    return jnp.asarray(m1, dtype=BF16), jnp.asarray(m2, dtype=BF16)


def _twiddles(n):
    seq_len = n * n
    idx = jnp.arange(n, dtype=jnp.int32)
    ang = ((idx[:, None] * idx[None, :]) % seq_len).astype(F32) * (2.0 * math.pi / seq_len)
    rep = lambda t: jnp.broadcast_to(t[:, :, None], (n, n, LANES))
    return rep(jnp.cos(ang)), rep(jnp.sin(ang))


def _in_projection(x4, w_qkvu, gain, m1):
    bsz, n, _, _ = x4.shape
    seq_len, tm, tiles = n * n, n * FAST, n // FAST
    cos, sin = _rope_tables(seq_len)
    twc, tws = _twiddles(n)
    pos = pl.BlockSpec((n, FAST, LANES), lambda b, j: (0, j, 0))
    tw = pl.BlockSpec((FAST, n, LANES), lambda b, j: (j, 0, 0))
    tr = lambda rows: pl.BlockSpec((None, rows, tm), lambda b, j: (b, 0, j))
    plane = pl.BlockSpec((None, FOURIER_GROUPS, n, FAST, FOURIER_GROUP_W), lambda b, j: (b, 0, 0, j, 0))
    return pl.pallas_call(
        _inproj_kernel,
        grid=(bsz, tiles),
        in_specs=[
            _tile_view(D_MODEL)(n),
            _resident((D_MODEL, QKVU_W)),
            _resident((1, ATTN_W + KV_W)),
            pos, pos,
            _resident((2 * LANES, 2 * LANES)),
            _resident((FOURIER_GROUP_W, 2 * FOURIER_GROUP_W)),
            _resident((2 * n, 2 * n)),
            tw, tw,
        ],
        out_specs=[
            tr(ATTN_W),
            pl.BlockSpec((tm, KV_W), lambda b, j: (b * tiles + j, 0)),
            tr(KV_W),
            plane, plane,
        ],
        out_shape=[
            jax.ShapeDtypeStruct((bsz, ATTN_W, seq_len), BF16),
            jax.ShapeDtypeStruct((bsz * seq_len, KV_W), BF16),
            jax.ShapeDtypeStruct((bsz, KV_W, seq_len), BF16),
            jax.ShapeDtypeStruct((bsz, FOURIER_GROUPS, n, n, FOURIER_GROUP_W), F32),
            jax.ShapeDtypeStruct((bsz, FOURIER_GROUPS, n, n, FOURIER_GROUP_W), F32),
        ],
        scratch_shapes=[pltpu.VMEM((FOURIER_GROUPS, tm, FOURIER_GROUP_W), F32)] * 4,
        compiler_params=_params(("parallel", "parallel")),
        name="in_projection",
    )(x4, w_qkvu, gain, cos.reshape(n, n, LANES), sin.reshape(n, n, LANES),
      _segment_mean_matrix(), _channel_dft_matrix(), m1, twc, tws)


SUM_ROWS = 16
KEY_SUBTILE = 256
SAFE_SCORE_BOUND = 60.0
BOUND_SLACK = 1.05
BOUNDED_GROUP_KEYS = 8192


def _attention_kernel(bound_ref, qt_ref, k_ref, vt_ref, o_ref, qpad_ref, sa_ref, sb_ref, acc_ref, *,
                      tq, tk, unroll):
    n_tiles = qt_ref.shape[-1] // tq
    seq_len = k_ref.shape[0]
    n_chunks = seq_len // tk
    assert unroll % 2 == 0 and n_chunks % unroll == 0 and tk % KEY_SUBTILE == 0
    kv_head = pl.program_id(1)
    row_head = lax.broadcasted_iota(jnp.int32, (KV_W, tq), 0) // HEAD_DIM
    ones = jnp.ones((SUM_ROWS, tk), BF16)
    bufs = (sa_ref, sb_ref)

    def load_queries(tile, slot):
        cols = pl.ds(pl.multiple_of(tile * tq, tq), tq)
        for h in range(GROUP):
            qh = qt_ref[h * HEAD_DIM:(h + 1) * HEAD_DIM, cols]
            q_rep = jnp.concatenate([qh] * N_KV_HEADS, axis=0)
            qpad_ref[slot, h] = jnp.where(row_head == kv_head, q_rep, jnp.zeros_like(q_rep))

    def key_chunk(c):
        return pl.ds(pl.multiple_of(c * tk, tk), tk)

    def step(q_slot, c_next, s_next_ref, c_cur, s_cur_ref, m_run, m_chunk):
        kc = k_ref[key_chunk(c_next), :]
        if c_cur is not None:
            vc = jnp.concatenate([vt_ref[:, key_chunk(c_cur)], ones], axis=0)
        m_next, maxima = [], []
        for h in range(GROUP):
            if c_cur is not None:
                m_new = jnp.maximum(m_run[h], m_chunk[h])
                alpha = jnp.exp2(m_run[h] - m_new)
                m_next.append(m_new)
            pv, mx = None, None
            for t in range(0, tk, KEY_SUBTILE):
                s = _dot(kc[t:t + KEY_SUBTILE], qpad_ref[q_slot, h])
                s_next_ref[h, t:t + KEY_SUBTILE, :] = s
                smax = jnp.max(s, axis=0, keepdims=True)
                mx = smax if mx is None else jnp.maximum(mx, smax)
                if c_cur is not None:
                    p = jnp.exp2(s_cur_ref[h, t:t + KEY_SUBTILE, :] - m_new).astype(BF16)
                    d = _dot(vc[:, t:t + KEY_SUBTILE], p)
                    pv = d if pv is None else pv + d
            if c_cur is not None:
                acc_ref[h] = alpha * acc_ref[h] + pv
            maxima.append(mx)
        return tuple(m_next), tuple(maxima)

    def tile_body(tile, m_chunk):
        slot = tile % 2
        load_queries(jnp.minimum(tile + 1, n_tiles - 1), 1 - slot)
        acc_ref[...] = jnp.zeros_like(acc_ref)
        m_run = tuple(jnp.full((1, tq), -jnp.inf, F32) for _ in range(GROUP))

        def group(j, carry):
            m_run, m_chunk = carry
            for u in range(unroll):
                c = unroll * j + u
                m_run, m_chunk = step(slot, c + 1, bufs[(u + 1) % 2], c, bufs[u % 2], m_run, m_chunk)
            return m_run, m_chunk

        m_run, m_chunk = lax.fori_loop(0, n_chunks // unroll - 1, group, (m_run, m_chunk))
        for c in range(n_chunks - unroll, n_chunks - 1):
            m_run, m_chunk = step(slot, c + 1, bufs[(c + 1) % 2], c, bufs[c % 2], m_run, m_chunk)
        _, m_chunk = step(1 - slot, 0, bufs[0], n_chunks - 1, bufs[(n_chunks - 1) % 2], m_run, m_chunk)
        write_tile(tile, partial_sums=False)
        return m_chunk

    def write_tile(tile, partial_sums):
        outs = []
        for h in range(GROUP):
            a = acc_ref[h]
            total = (jnp.sum(a[HEAD_DIM:HEAD_DIM + SUBLANES], axis=0, keepdims=True) if partial_sums
                     else a[HEAD_DIM:HEAD_DIM + 1])
            outs.append(a[:HEAD_DIM] / total)
        rows = pl.ds(pl.multiple_of(tile * tq, tq), tq)
        o_ref[rows, :] = jnp.concatenate(outs, axis=0).T.astype(BF16)

    group_keys = min(seq_len, BOUNDED_GROUP_KEYS)
    n_sub = group_keys // KEY_SUBTILE

    def bounded_tile(tile, carry):
        load_queries(tile, 0)
        acc_ref[...] = jnp.zeros_like(acc_ref)

        def group(j, carry):
            def keys(t):
                return pl.ds(pl.multiple_of(j * group_keys + t * KEY_SUBTILE, KEY_SUBTILE), KEY_SUBTILE)
            pv = [None] * GROUP
            psum = [None] * GROUP
            s_prev = [None] * GROUP
            for t in range(n_sub + 1):
                if t < n_sub:
                    kt = k_ref[keys(t), :]
                if t > 0:
                    vt = vt_ref[:, keys(t - 1)]
                for h in range(GROUP):
                    s_cur = _dot(kt, qpad_ref[0, h]) if t < n_sub else None
                    if t > 0:
                        p = jnp.exp2(s_prev[h])
                        d = _dot(vt, p.astype(BF16))
                        ps = jnp.sum(p.reshape(KEY_SUBTILE // SUBLANES, SUBLANES, tq), axis=0)
                        pv[h] = d if pv[h] is None else pv[h] + d
                        psum[h] = ps if psum[h] is None else psum[h] + ps
                    s_prev[h] = s_cur
            for h in range(GROUP):
                acc_ref[h, :HEAD_DIM, :] += pv[h]
                acc_ref[h, HEAD_DIM:HEAD_DIM + SUBLANES, :] += psum[h]
            return carry

        lax.fori_loop(0, seq_len // group_keys, group, 0)
        write_tile(tile, partial_sums=True)
        return carry

    bounded = bound_ref[0] <= SAFE_SCORE_BOUND

    @pl.when(bounded)
    def _():
        lax.fori_loop(0, n_tiles, bounded_tile, 0)

    @pl.when(jnp.logical_not(bounded))
    def _():
        load_queries(0, 0)
        _, m_chunk = step(0, 0, bufs[0], None, None, None, None)
        lax.fori_loop(0, n_tiles, tile_body, m_chunk)


def _attention(score_bound, qt, k, vt, tq, tk, unroll, tiles_per_step):
    bsz, _, seq_len = qt.shape
    gw = GROUP * HEAD_DIM
    tqb = tq * tiles_per_step
    assert seq_len % tqb == 0
    return pl.pallas_call(
        functools.partial(_attention_kernel, tq=tq, tk=tk, unroll=unroll),
        grid=(bsz, N_KV_HEADS, seq_len // tqb),
        in_specs=[
            pl.BlockSpec(memory_space=pltpu.SMEM),
            pl.BlockSpec((None, gw, tqb), lambda b, g, i: (b, g, i)),
            pl.BlockSpec((None, seq_len, KV_W), lambda b, g, i: (b, 0, 0)),
            pl.BlockSpec((None, HEAD_DIM, seq_len), lambda b, g, i: (b, g, 0)),
        ],
        out_specs=pl.BlockSpec((None, tqb, gw), lambda b, g, i: (b, i, g)),
        out_shape=jax.ShapeDtypeStruct((bsz, seq_len, ATTN_W), BF16),
        scratch_shapes=[pltpu.VMEM((2, GROUP, KV_W, tq), BF16),
                        pltpu.VMEM((GROUP, tk, tq), F32),
                        pltpu.VMEM((GROUP, tk, tq), F32),
                        pltpu.VMEM((GROUP, HEAD_DIM + SUM_ROWS, tq), F32)],
        compiler_params=_params(("parallel", "parallel", "parallel")),
        name="gqa_attention",
    )(score_bound, qt, k, vt)


def _mem_kv_kernel(mem_ref, w_ref, k_ref, v_ref):
    y = _dot(mem_ref[...].astype(BF16), w_ref[...])
    k_ref[...] = y[:, :D_MODEL].astype(BF16)
    v_ref[...] = y[:, D_MODEL:].astype(BF16)


def _mem_kv(mem2d, w_kv):
    rows = mem2d.shape[0]
    blk = pl.BlockSpec((MEM_TOKENS, D_MODEL), lambda i: (i, 0))
    return pl.pallas_call(
        _mem_kv_kernel,
        grid=(rows // MEM_TOKENS,),
        in_specs=[blk, _resident((D_MODEL, 2 * D_MODEL))],
        out_specs=[blk, blk],
        out_shape=[jax.ShapeDtypeStruct((rows, D_MODEL), BF16)] * 2,
        compiler_params=_params(("parallel",)),
        name="mem_kv",
    )(mem2d, w_kv)


MIX_ROWS = 512
MIX_PARTS = 2


def _mix_xattn_kernel(x_ref, o_ref, yr_ref, yi_ref, m2_ref, wg_ref, wab_ref, wfb_ref, wmix_ref,
                      ln1g_ref, ln1b_ref, wq_ref, km_ref, vm_ref, wo_ref, ln2g_ref, ln2b_ref,
                      y_ref, f_scr, *, alpha):
    n = x_ref.shape[0]
    tm = n * FAST
    m2 = m2_ref[...]
    for r in range(FAST):
        plane = lambda ref: jnp.concatenate([ref[gi, r] for gi in range(FOURIER_GROUPS)], axis=-1)
        y2 = jnp.concatenate([plane(yr_ref), plane(yi_ref)], axis=0).astype(BF16)
        f = _dot(m2, y2)
        for gi in range(FOURIER_GROUPS):
            f_scr.at[gi][pl.ds(r, n, stride=FAST), :] = f[:, gi * FOURIER_GROUP_W:(gi + 1) * FOURIER_GROUP_W]

    pass_rows = min(MIX_ROWS, tm)

    n_parts = MIX_PARTS if pass_rows % (MIX_PARTS * FAST * SUBLANES) == 0 else 1
    part = pass_rows // n_parts

    def dense(i):
        def rows(g):
            return pl.ds(pl.multiple_of(i * pass_rows + g * part, part), part)

        def merge(g):
            slow = pl.ds(pl.multiple_of((i * pass_rows + g * part) // FAST, part // FAST), part // FAST)
            x = x_ref[slow].reshape(part, D_MODEL)
            gates = _dot(x.astype(BF16), wg_ref[...])
            y_attn = _dot(o_ref[rows(g), :], wab_ref[...])
            f = jnp.concatenate([f_scr[gi, rows(g), :] for gi in range(FOURIER_GROUPS)], axis=-1)
            y_four = _dot(f.astype(BF16), wfb_ref[...])
            merged = (jax.nn.sigmoid(gates[:, :D_MODEL]) * y_attn
                      + jax.nn.sigmoid(gates[:, D_MODEL:]) * y_four)
            return _layernorm(alpha * x + _dot(merged.astype(BF16), wmix_ref[...]), ln1g_ref[...], ln1b_ref[...])

        def cross_attend(x1):
            q = (_dot(x1.astype(BF16), wq_ref[...]) * (MEM_HEAD_DIM ** -0.5 * LOG2E)).astype(BF16)
            heads = []
            for h in range(MEM_HEADS):
                c = h * MEM_HEAD_DIM
                s = _dot_nt(q[:, c:c + MEM_HEAD_DIM], km_ref[:, c:c + MEM_HEAD_DIM])
                p = jnp.exp2(s - jnp.max(s, axis=-1, keepdims=True))
                l = jnp.sum(p, axis=-1, keepdims=True)
                heads.append(_dot(p.astype(BF16), vm_ref[:, c:c + MEM_HEAD_DIM]) / l)
            return jnp.concatenate(heads, axis=-1).astype(BF16)

        x1 = [merge(g) for g in range(n_parts)]
        ctx = [cross_attend(x1[g]) for g in range(n_parts)]
        for g in range(n_parts):
            y_ref[rows(g), :] = _layernorm(alpha * x1[g] + _dot(ctx[g], wo_ref[...]), ln2g_ref[...], ln2b_ref[...])

    assert tm % pass_rows == 0
    if tm == pass_rows:
        dense(0)
    else:
        def body(i, carry):
            dense(i)
            return carry
        lax.fori_loop(0, tm // pass_rows, body, 0)


def _mix_xattn(x4, o2d, yr, yi, m2, wg, wab, wfb, wmix, ln1g, ln1b, wq, km, vm, wo, ln2g, ln2b, alpha):
    bsz, n, _, _ = x4.shape
    tm, tiles = n * FAST, n // FAST
    tok = lambda w: pl.BlockSpec((tm, w), lambda b, j: (b * tiles + j, 0))
    plane = pl.BlockSpec((None, FOURIER_GROUPS, FAST, n, FOURIER_GROUP_W), lambda b, j: (b, 0, j, 0, 0))
    mem = pl.BlockSpec((MEM_TOKENS, D_MODEL), lambda b, j: (b, 0))
    vec = _resident((1, D_MODEL))
    return pl.pallas_call(
        functools.partial(_mix_xattn_kernel, alpha=alpha),
        grid=(bsz, tiles),
        in_specs=[_tile_view(D_MODEL)(n), tok(ATTN_W), plane, plane, _resident((n, 2 * n)),
                  _resident((D_MODEL, 2 * D_MODEL)), _resident((ATTN_W, D_MODEL)),
                  _resident((FOURIER_W, D_MODEL)), _resident((D_MODEL, D_MODEL)), vec, vec,
                  _resident((D_MODEL, D_MODEL)), mem, mem, _resident((D_MODEL, D_MODEL)), vec, vec],
        out_specs=tok(D_MODEL),
        out_shape=jax.ShapeDtypeStruct((bsz * n * n, D_MODEL), F32),
        scratch_shapes=[pltpu.VMEM((FOURIER_GROUPS, tm, FOURIER_GROUP_W), F32)],
        compiler_params=_params(("parallel", "parallel")),
        name="mix_xattn",
    )(x4, o2d, yr, yi, m2, wg, wab, wfb, wmix, ln1g, ln1b, wq, km, vm, wo, ln2g, ln2b)


FF_CHUNK = 1024


def _mlp_kernel(x_ref, wup_ref, wdown_ref, g_ref, b_ref, y_ref, *, alpha):
    x = x_ref[...]
    xb = x.astype(BF16)
    y = alpha * x
    for c in range(0, D_FF, FF_CHUNK):
        h = jnp.maximum(_dot(xb, wup_ref[:, c:c + FF_CHUNK]), 0.0)
        y = y + _dot((h * h).astype(BF16), wdown_ref[c:c + FF_CHUNK, :])
    y_ref[...] = _layernorm(y, g_ref[...], b_ref[...]).reshape(y_ref.shape)


def _mlp(x2d, bsz, n, wup, wdown, g, b, alpha):
    tm, tiles = n * FAST, n // FAST
    vec = _resident((1, D_MODEL))
    return pl.pallas_call(
        functools.partial(_mlp_kernel, alpha=alpha),
        grid=(bsz, tiles),
        in_specs=[pl.BlockSpec((tm, D_MODEL), lambda b, j: (b * tiles + j, 0)),
                  _resident((D_MODEL, D_FF)), _resident((D_FF, D_MODEL)), vec, vec],
        out_specs=_tile_view(D_MODEL)(n),
        out_shape=jax.ShapeDtypeStruct((bsz, n, n, D_MODEL), F32),
        compiler_params=_params(("parallel", "parallel")),
        name="mlp",
    )(x2d, wup, wdown, g, b)


def _tile(n, want):
    t = min(n, want)
    assert n % t == 0
    return t


def _layer(x, mem, w, alpha):
    bsz, seq_len, _ = x.shape
    n = _grid_side(seq_len)
    x4 = x.reshape(bsz, n, n, D_MODEL)
    m1, m2 = _dft_matrices(n)
    qt, k, vt, yr, yi = _in_projection(x4, w["qkvu"], w["qk_gain"], m1)
    tq = _tile(seq_len, 512)
    o = _attention(w["score_bound"], qt, k.reshape(bsz, seq_len, KV_W), vt, tq, _tile(seq_len, 512),
                   unroll=4 if seq_len % 8192 == 0 else 2, tiles_per_step=_tile(seq_len // tq, 16))
    km, vm = _mem_kv(mem.reshape(bsz * MEM_TOKENS, D_MODEL), w["mem_kv"])
    x2 = _mix_xattn(x4, o.reshape(bsz * seq_len, ATTN_W), yr, yi, m2, w["gate"], w["attn_branch"],
                    w["fourier_branch"], w["mix_out"], w["ln1_g"], w["ln1_b"], w["mem_q"], km, vm,
                    w["mem_o"], w["ln2_g"], w["ln2_b"], alpha)
    y = _mlp(x2, bsz, n, w["up"], w["down"], w["ln3_g"], w["ln3_b"], alpha)
    return y.reshape(bsz, seq_len, D_MODEL)


def kernel(x_prompt, x_sample, mem_prompt, mem_sample, w_in, q_norm, k_norm, w_attn_branch, w_fourier_branch, w_mix_out, ln1_g, ln1_b, w_mem_q, w_mem_k, w_mem_v, w_mem_o, ln2_g, ln2_b, w_up, w_down, ln3_g, ln3_b):
    depth = w_in.shape[0]
    alpha = float((2 * depth) ** 0.25)
    y_prompt, y_sample = x_prompt, x_sample
    for l in range(depth):
        row = lambda v: v[l].reshape(1, -1).astype(F32)
        w = {
            "qkvu": w_in[l, :, :QKVU_W].astype(BF16),
            "gate": w_in[l, :, QKVU_W:].astype(BF16),
            "qk_gain": jnp.concatenate([jnp.tile(q_norm[l], N_HEADS), jnp.tile(k_norm[l], N_KV_HEADS)]).reshape(1, -1).astype(F32),
            "score_bound": (HEAD_DIM ** 0.5 * LOG2E * BOUND_SLACK * jnp.max(jnp.abs(q_norm[l]))
                            * jnp.max(jnp.abs(k_norm[l]))).reshape(1).astype(F32),
            "attn_branch": w_attn_branch[l].astype(BF16),
            "fourier_branch": w_fourier_branch[l].astype(BF16),
            "mix_out": w_mix_out[l].astype(BF16),
            "ln1_g": row(ln1_g), "ln1_b": row(ln1_b),
            "mem_q": w_mem_q[l].astype(BF16),
            "mem_kv": jnp.concatenate([w_mem_k[l], w_mem_v[l]], axis=1).astype(BF16),
            "mem_o": w_mem_o[l].astype(BF16),
            "ln2_g": row(ln2_g), "ln2_b": row(ln2_b),
            "up": w_up[l].astype(BF16), "down": w_down[l].astype(BF16),
            "ln3_g": row(ln3_g), "ln3_b": row(ln3_b),
        }
        y_prompt = _layer(y_prompt, mem_prompt, w, alpha)
        y_sample = _layer(y_sample, mem_sample, w, alpha)
    return (y_prompt, y_sample)
```

```python
import functools
import math

import numpy as np
import jax
import jax.numpy as jnp
from jax import lax
from jax.experimental import pallas as pl
from jax.experimental.pallas import tpu as pltpu

D_MODEL = 1024
GRID_W = 64
N_HEADS = 8
N_KV_HEADS = 2
HEAD_DIM = 64
GROUP = N_HEADS // N_KV_HEADS
ATTN_W = N_HEADS * HEAD_DIM
KV_W = N_KV_HEADS * HEAD_DIM
ROPE_THETA = 10000.0
AXIS_ROT = HEAD_DIM // 2
FOURIER_GROUPS = 4
FOURIER_GROUP_W = 128
FOURIER_W = FOURIER_GROUPS * FOURIER_GROUP_W
QKVU_W = ATTN_W + 2 * KV_W + FOURIER_W
MEM_TOKENS = 256
MEM_HEADS = 4
MEM_HEAD_DIM = D_MODEL // MEM_HEADS
D_FF = 4 * D_MODEL
RMS_EPS = 1e-6
LN_EPS = 1e-5
LOG2E = math.log2(math.e)

V7X_VMEM_BYTES = 64 * 1024 * 1024
VMEM_LIMIT_BYTES = V7X_VMEM_BYTES - 12 * 1024 * 1024
LANES = 128
SUBLANES = 8

BF16 = jnp.bfloat16
F32 = jnp.float32


def _dot(a, b):
    return jnp.dot(a, b, preferred_element_type=F32)


def _resident(shape):
    n = len(shape)
    return pl.BlockSpec(shape, lambda *_: (0,) * n, pipeline_mode=pl.Buffered(1))


def _params(semantics):
    return pltpu.CompilerParams(dimension_semantics=semantics, vmem_limit_bytes=VMEM_LIMIT_BYTES)


def _layernorm(x, g, b):
    mu = jnp.mean(x, axis=-1, keepdims=True)
    xc = x - mu
    var = jnp.mean(xc * xc, axis=-1, keepdims=True)
    return xc * lax.rsqrt(var + LN_EPS) * g + b


TILE_TOKENS = 512


def _grid_side(seq_len):
    n = math.isqrt(seq_len)
    fast = max(SUBLANES, TILE_TOKENS // n)
    assert n * n == seq_len and fast % SUBLANES == 0 and n % fast == 0, "unsupported sequence length"
    return n, fast


def _tile_view(n, fast, width):
    return pl.BlockSpec((None, n, fast, width), lambda b, j: (b, 0, j, 0))


def _inproj_kernel(x_ref, w_ref, gain_ref, cos_ref, sin_ref, seg_ref, cdft_ref, m1_ref, twc_ref, tws_ref,
                   qt_ref, k_ref, vt_ref, yr_ref, yi_ref, zc_scr, zs_scr, yr_scr, yi_scr):
    n, fast = x_ref.shape[0], x_ref.shape[1]
    tm = n * fast
    xb = x_ref[...].reshape(tm, D_MODEL).astype(BF16)
    h = _dot(xb, w_ref[...])
    qk_w = ATTN_W + KV_W
    qk = h[:, :qk_w]
    sq = (qk * qk).astype(BF16)
    seg = seg_ref[...]
    ms = jnp.concatenate(
        [_dot(sq[:, c:c + 2 * LANES], seg) for c in range(0, ATTN_W, 2 * LANES)]
        + [_dot(sq[:, ATTN_W:qk_w], seg[:KV_W, :KV_W])], axis=-1)
    qkn = qk * lax.rsqrt(ms + RMS_EPS) * gain_ref[...]
    cos = cos_ref[...].reshape(tm, LANES)
    sin = sin_ref[...].reshape(tm, LANES)
    lane = lax.broadcasted_iota(jnp.int32, cos.shape, 1)
    first_half = (lane % HEAD_DIM) < (HEAD_DIM // 2)
    slabs = []
    for c in range(0, qk_w, LANES):
        xs = qkn[:, c:c + LANES]
        ahead = pltpu.roll(xs, LANES - HEAD_DIM // 2, 1)
        behind = pltpu.roll(xs, HEAD_DIM // 2, 1)
        slabs.append(xs * cos + jnp.where(first_half, ahead, behind) * sin)
    q = jnp.concatenate(slabs[:ATTN_W // LANES], axis=-1)
    qt_ref[...] = q.T.astype(BF16)
    k_ref[...] = slabs[-1].astype(BF16)
    vt_ref[...] = h[:, qk_w:qk_w + KV_W].T.astype(BF16)
    ub = h[:, qk_w + KV_W:].astype(BF16)
    cdft = cdft_ref[...]
    for gi in range(FOURIER_GROUPS):
        c = gi * FOURIER_GROUP_W
        z = _dot(ub[:, c:c + FOURIER_GROUP_W], cdft)
        zc_scr[gi] = z[:, :FOURIER_GROUP_W]
        zs_scr[gi] = z[:, FOURIER_GROUP_W:]
    m1 = m1_ref[...]
    reps = FOURIER_W // LANES
    for r in range(fast):
        rows = pl.ds(r, n, stride=fast)
        gather = lambda scr: jnp.concatenate([scr.at[gi][rows, :] for gi in range(FOURIER_GROUPS)], axis=-1)
        x2 = jnp.concatenate([gather(zc_scr), gather(zs_scr)], axis=0).astype(BF16)
        y = _dot(m1, x2)
        yr, yi = y[:n], y[n:]
        c = jnp.concatenate([twc_ref[r]] * reps, axis=-1)
        s = jnp.concatenate([tws_ref[r]] * reps, axis=-1)
        for scr, val in ((yr_scr, yr * c + yi * s), (yi_scr, yi * c - yr * s)):
            for gi in range(FOURIER_GROUPS):
                scr.at[gi][rows, :] = val[:, gi * FOURIER_GROUP_W:(gi + 1) * FOURIER_GROUP_W]
    for gi in range(FOURIER_GROUPS):
        yr_ref[gi] = yr_scr[gi].reshape(n, fast, FOURIER_GROUP_W)
        yi_ref[gi] = yi_scr[gi].reshape(n, fast, FOURIER_GROUP_W)


def _rope_tables(seq_len):
    rows = seq_len // GRID_W
    row = jnp.repeat(jnp.arange(rows, dtype=F32), GRID_W)
    col = jnp.tile(jnp.arange(GRID_W, dtype=F32), rows)
    freqs = ROPE_THETA ** (-jnp.arange(0, AXIS_ROT, 2, dtype=F32) / AXIS_ROT)
    ang = jnp.concatenate([row[:, None] * freqs, col[:, None] * freqs], axis=-1)
    ang = jnp.concatenate([ang, ang], axis=-1)
    cos, sin = jnp.cos(ang), jnp.sin(ang)
    sign = jnp.where(jnp.arange(HEAD_DIM) < HEAD_DIM // 2, -1.0, 1.0).astype(F32)
    reps = LANES // HEAD_DIM
    return jnp.tile(cos, (1, reps)), jnp.tile(sin * sign, (1, reps))


def _segment_mean_matrix():
    idx = np.arange(2 * LANES)
    m = (idx[:, None] // HEAD_DIM == idx[None, :] // HEAD_DIM).astype(np.float32) / HEAD_DIM
    return jnp.asarray(m, dtype=BF16)


def _channel_dft_matrix():
    n = FOURIER_GROUP_W
    jk = np.outer(np.arange(n), np.arange(n)) % n
    ang = 2.0 * np.pi * jk / n
    m = np.concatenate([np.cos(ang), np.sin(ang)], axis=1) / math.sqrt(n)
    return jnp.asarray(m, dtype=BF16)


def _dft_matrices(n):
    ang = 2.0 * np.pi * (np.outer(np.arange(n), np.arange(n)) % n) / n
    c, s = np.cos(ang) / math.sqrt(n), np.sin(ang) / math.sqrt(n)
    m1 = np.block([[c, -s], [-s, -c]])
    m2 = np.concatenate([c, s], axis=1)
    return jnp.asarray(m1, dtype=BF16), jnp.asarray(m2, dtype=BF16)


def _twiddles(n):
    seq_len = n * n
    idx = jnp.arange(n, dtype=jnp.int32)
    ang = ((idx[:, None] * idx[None, :]) % seq_len).astype(F32) * (2.0 * math.pi / seq_len)
    rep = lambda t: jnp.broadcast_to(t[:, :, None], (n, n, LANES))
    return rep(jnp.cos(ang)), rep(jnp.sin(ang))


def _in_projection(x4, fast, w_qkvu, gain, m1):
    bsz, n, _, _ = x4.shape
    seq_len, tm, tiles = n * n, n * fast, n // fast
    cos, sin = _rope_tables(seq_len)
    twc, tws = _twiddles(n)
    pos = pl.BlockSpec((n, fast, LANES), lambda b, j: (0, j, 0))
    tw = pl.BlockSpec((fast, n, LANES), lambda b, j: (j, 0, 0))
    tr = lambda rows: pl.BlockSpec((None, rows, tm), lambda b, j: (b, 0, j))
    plane = pl.BlockSpec((None, FOURIER_GROUPS, n, fast, FOURIER_GROUP_W), lambda b, j: (b, 0, 0, j, 0))
    return pl.pallas_call(
        _inproj_kernel,
        grid=(bsz, tiles),
        in_specs=[
            _tile_view(n, fast, D_MODEL),
            _resident((D_MODEL, QKVU_W)),
            _resident((1, ATTN_W + KV_W)),
            pos, pos,
            _resident((2 * LANES, 2 * LANES)),
            _resident((FOURIER_GROUP_W, 2 * FOURIER_GROUP_W)),
            _resident((2 * n, 2 * n)),
            tw, tw,
        ],
        out_specs=[
            tr(ATTN_W),
            pl.BlockSpec((tm, KV_W), lambda b, j: (b * tiles + j, 0)),
            tr(KV_W),
            plane, plane,
        ],
        out_shape=[
            jax.ShapeDtypeStruct((bsz, ATTN_W, seq_len), BF16),
            jax.ShapeDtypeStruct((bsz * seq_len, KV_W), BF16),
            jax.ShapeDtypeStruct((bsz, KV_W, seq_len), BF16),
            jax.ShapeDtypeStruct((bsz, FOURIER_GROUPS, n, n, FOURIER_GROUP_W), F32),
            jax.ShapeDtypeStruct((bsz, FOURIER_GROUPS, n, n, FOURIER_GROUP_W), F32),
        ],
        scratch_shapes=[pltpu.VMEM((FOURIER_GROUPS, tm, FOURIER_GROUP_W), F32)] * 4,
        compiler_params=_params(("parallel", "parallel")),
        name="in_projection",
    )(x4, w_qkvu, gain, cos.reshape(n, n, LANES), sin.reshape(n, n, LANES),
      _segment_mean_matrix(), _channel_dft_matrix(), m1, twc, tws)


SUM_ROWS = 16
KEY_SUBTILE = 256
SAFE_SCORE_BOUND = 60.0
BOUND_SLACK = 1.05
BOUNDED_GROUP_KEYS = 16384


def _attention_kernel(bound_ref, qt_ref, k_ref, vt_ref, o_ref, qpad_ref, sa_ref, sb_ref, acc_ref, *,
                      tq, tk, unroll):
    n_tiles = qt_ref.shape[-1] // tq
    seq_len = k_ref.shape[0]
    n_chunks = seq_len // tk
    assert unroll % 2 == 0 and n_chunks % unroll == 0 and tk % KEY_SUBTILE == 0
    kv_head = pl.program_id(1)
    row_head = lax.broadcasted_iota(jnp.int32, (KV_W, tq), 0) // HEAD_DIM
    ones = jnp.ones((SUM_ROWS, tk), BF16)
    bufs = (sa_ref, sb_ref)

    def load_queries(tile, slot):
        cols = pl.ds(pl.multiple_of(tile * tq, tq), tq)
        for h in range(GROUP):
            qh = qt_ref[h * HEAD_DIM:(h + 1) * HEAD_DIM, cols]
            q_rep = jnp.concatenate([qh] * N_KV_HEADS, axis=0)
            qpad_ref[slot, h] = jnp.where(row_head == kv_head, q_rep, jnp.zeros_like(q_rep))

    def key_chunk(c):
        return pl.ds(pl.multiple_of(c * tk, tk), tk)

    def step(q_slot, c_next, s_next_ref, c_cur, s_cur_ref, m_run, m_chunk):
        kc = k_ref[key_chunk(c_next), :]
        if c_cur is not None:
            vc = jnp.concatenate([vt_ref[:, key_chunk(c_cur)], ones], axis=0)
        m_next, maxima = [], []
        for h in range(GROUP):
            if c_cur is not None:
                m_new = jnp.maximum(m_run[h], m_chunk[h])
                alpha = jnp.exp2(m_run[h] - m_new)
                m_next.append(m_new)
            pv, mx = None, None
            for t in range(0, tk, KEY_SUBTILE):
                s = _dot(kc[t:t + KEY_SUBTILE], qpad_ref[q_slot, h])
                s_next_ref[h, t:t + KEY_SUBTILE, :] = s
                smax = jnp.max(s, axis=0, keepdims=True)
                mx = smax if mx is None else jnp.maximum(mx, smax)
                if c_cur is not None:
                    p = jnp.exp2(s_cur_ref[h, t:t + KEY_SUBTILE, :] - m_new).astype(BF16)
                    d = _dot(vc[:, t:t + KEY_SUBTILE], p)
                    pv = d if pv is None else pv + d
            if c_cur is not None:
                acc_ref[h] = alpha * acc_ref[h] + pv
            maxima.append(mx)
        return tuple(m_next), tuple(maxima)

    def tile_body(tile, m_chunk):
        slot = tile % 2
        load_queries(jnp.minimum(tile + 1, n_tiles - 1), 1 - slot)
        acc_ref[...] = jnp.zeros_like(acc_ref)
        m_run = tuple(jnp.full((1, tq), -jnp.inf, F32) for _ in range(GROUP))

        def group(j, carry):
            m_run, m_chunk = carry
            for u in range(unroll):
                c = unroll * j + u
                m_run, m_chunk = step(slot, c + 1, bufs[(u + 1) % 2], c, bufs[u % 2], m_run, m_chunk)
            return m_run, m_chunk

        m_run, m_chunk = lax.fori_loop(0, n_chunks // unroll - 1, group, (m_run, m_chunk))
        for c in range(n_chunks - unroll, n_chunks - 1):
            m_run, m_chunk = step(slot, c + 1, bufs[(c + 1) % 2], c, bufs[c % 2], m_run, m_chunk)
        _, m_chunk = step(1 - slot, 0, bufs[0], n_chunks - 1, bufs[(n_chunks - 1) % 2], m_run, m_chunk)
        write_tile(tile, partial_sums=False)
        return m_chunk

    def write_tile(tile, partial_sums):
        outs = []
        for h in range(GROUP):
            a = acc_ref[h]
            total = (jnp.sum(a[HEAD_DIM:HEAD_DIM + SUBLANES], axis=0, keepdims=True) if partial_sums
                     else a[HEAD_DIM:HEAD_DIM + 1])
            outs.append(a[:HEAD_DIM] / total)
        rows = pl.ds(pl.multiple_of(tile * tq, tq), tq)
        o_ref[rows, :] = jnp.concatenate(outs, axis=0).T.astype(BF16)

    group_keys = min(seq_len, BOUNDED_GROUP_KEYS)
    n_sub = group_keys // KEY_SUBTILE

    def bounded_tile(tile, carry):
        load_queries(tile, 0)
        acc_ref[...] = jnp.zeros_like(acc_ref)

        def group(j, carry):
            def keys(t):
                return pl.ds(pl.multiple_of(j * group_keys + t * KEY_SUBTILE, KEY_SUBTILE), KEY_SUBTILE)
            pv = [None] * GROUP
            psum = [None] * GROUP
            s_prev = [None] * GROUP
            for t in range(n_sub + 1):
                if t < n_sub:
                    kt = k_ref[keys(t), :]
                if t > 0:
                    vt = vt_ref[:, keys(t - 1)]
                for h in range(GROUP):
                    s_cur = _dot(kt, qpad_ref[0, h]) if t < n_sub else None
                    if t > 0:
                        p = jnp.exp2(s_prev[h])
                        d = _dot(vt, p.astype(BF16))
                        ps = jnp.sum(p.reshape(KEY_SUBTILE // SUBLANES, SUBLANES, tq), axis=0)
                        pv[h] = d if pv[h] is None else pv[h] + d
                        psum[h] = ps if psum[h] is None else psum[h] + ps
                    s_prev[h] = s_cur
            for h in range(GROUP):
                acc_ref[h, :HEAD_DIM, :] += pv[h]
                acc_ref[h, HEAD_DIM:HEAD_DIM + SUBLANES, :] += psum[h]
            return carry

        lax.fori_loop(0, seq_len // group_keys, group, 0)
        write_tile(tile, partial_sums=True)
        return carry

    bounded = bound_ref[0] <= SAFE_SCORE_BOUND

    @pl.when(bounded)
    def _():
        lax.fori_loop(0, n_tiles, bounded_tile, 0)

    @pl.when(jnp.logical_not(bounded))
    def _():
        load_queries(0, 0)
        _, m_chunk = step(0, 0, bufs[0], None, None, None, None)
        lax.fori_loop(0, n_tiles, tile_body, m_chunk)


def _attention(score_bound, qt, k, vt, tq, tk, unroll, tiles_per_step):
    bsz, _, seq_len = qt.shape
    gw = GROUP * HEAD_DIM
    tqb = tq * tiles_per_step
    assert seq_len % tqb == 0
    return pl.pallas_call(
        functools.partial(_attention_kernel, tq=tq, tk=tk, unroll=unroll),
        grid=(bsz, N_KV_HEADS, seq_len // tqb),
        in_specs=[
            pl.BlockSpec(memory_space=pltpu.SMEM),
            pl.BlockSpec((None, gw, tqb), lambda b, g, i: (b, g, i)),
            pl.BlockSpec((None, seq_len, KV_W), lambda b, g, i: (b, 0, 0)),
            pl.BlockSpec((None, HEAD_DIM, seq_len), lambda b, g, i: (b, g, 0)),
        ],
        out_specs=pl.BlockSpec((None, tqb, gw), lambda b, g, i: (b, i, g)),
        out_shape=jax.ShapeDtypeStruct((bsz, seq_len, ATTN_W), BF16),
        scratch_shapes=[pltpu.VMEM((2, GROUP, KV_W, tq), BF16),
                        pltpu.VMEM((GROUP, tk, tq), F32),
                        pltpu.VMEM((GROUP, tk, tq), F32),
                        pltpu.VMEM((GROUP, HEAD_DIM + SUM_ROWS, tq), F32)],
        compiler_params=_params(("parallel", "parallel", "parallel")),
        name="gqa_attention",
    )(score_bound, qt, k, vt)


def _mem_kv_kernel(mem_ref, w_ref, kt_ref, v_ref):
    y = _dot(mem_ref[...].astype(BF16), w_ref[...])
    kt_ref[...] = y[:, :D_MODEL].T.astype(BF16)
    v_ref[...] = y[:, D_MODEL:].astype(BF16)


def _mem_kv(mem2d, w_kv):
    rows = mem2d.shape[0]
    blk = pl.BlockSpec((MEM_TOKENS, D_MODEL), lambda i: (i, 0))
    blk_t = pl.BlockSpec((D_MODEL, MEM_TOKENS), lambda i: (i, 0))
    return pl.pallas_call(
        _mem_kv_kernel,
        grid=(rows // MEM_TOKENS,),
        in_specs=[blk, _resident((D_MODEL, 2 * D_MODEL))],
        out_specs=[blk_t, blk],
        out_shape=[jax.ShapeDtypeStruct((rows // MEM_TOKENS * D_MODEL, MEM_TOKENS), BF16),
                   jax.ShapeDtypeStruct((rows, D_MODEL), BF16)],
        compiler_params=_params(("parallel",)),
        name="mem_kv",
    )(mem2d, w_kv)


MIX_ROWS = 512
MIX_PARTS = 2


def _mix_xattn_kernel(x_ref, o_ref, yr_ref, yi_ref, m2_ref, wg_ref, wab_ref, wfb_ref, wmix_ref,
                      ln1g_ref, ln1b_ref, wq_ref, kmt_ref, vm_ref, wo_ref, ln2g_ref, ln2b_ref,
                      y_ref, f_scr, *, alpha):
    n, fast = x_ref.shape[0], x_ref.shape[1]
    tm = n * fast
    m2 = m2_ref[...]
    for r in range(fast):
        plane = lambda ref: jnp.concatenate([ref[gi, r] for gi in range(FOURIER_GROUPS)], axis=-1)
        y2 = jnp.concatenate([plane(yr_ref), plane(yi_ref)], axis=0).astype(BF16)
        f = _dot(m2, y2)
        for gi in range(FOURIER_GROUPS):
            f_scr.at[gi][pl.ds(r, n, stride=fast), :] = f[:, gi * FOURIER_GROUP_W:(gi + 1) * FOURIER_GROUP_W]

    pass_rows = min(MIX_ROWS, tm)

    n_parts = MIX_PARTS if pass_rows % (MIX_PARTS * fast * SUBLANES) == 0 else 1
    part = pass_rows // n_parts

    def dense(i):
        def rows(g):
            return pl.ds(pl.multiple_of(i * pass_rows + g * part, part), part)

        def merge(g):
            slow = pl.ds(pl.multiple_of((i * pass_rows + g * part) // fast, part // fast), part // fast)
            x = x_ref[slow].reshape(part, D_MODEL)
            gates = _dot(x.astype(BF16), wg_ref[...])
            y_attn = _dot(o_ref[rows(g), :], wab_ref[...])
            f = jnp.concatenate([f_scr[gi, rows(g), :] for gi in range(FOURIER_GROUPS)], axis=-1)
            y_four = _dot(f.astype(BF16), wfb_ref[...])
            merged = (jax.nn.sigmoid(gates[:, :D_MODEL]) * y_attn
                      + jax.nn.sigmoid(gates[:, D_MODEL:]) * y_four)
            return _layernorm(alpha * x + _dot(merged.astype(BF16), wmix_ref[...]), ln1g_ref[...], ln1b_ref[...])

        def cross_attend(x1):
            q = (_dot(x1.astype(BF16), wq_ref[...]) * (MEM_HEAD_DIM ** -0.5 * LOG2E)).astype(BF16)
            heads = []
            for h in range(MEM_HEADS):
                c = h * MEM_HEAD_DIM
                s = _dot(q[:, c:c + MEM_HEAD_DIM], kmt_ref[c:c + MEM_HEAD_DIM, :])
                p = jnp.exp2(s - jnp.max(s, axis=-1, keepdims=True))
                l = jnp.sum(p, axis=-1, keepdims=True)
                heads.append(_dot(p.astype(BF16), vm_ref[:, c:c + MEM_HEAD_DIM]) / l)
            return jnp.concatenate(heads, axis=-1).astype(BF16)

        x1 = [merge(g) for g in range(n_parts)]
        ctx = [cross_attend(x1[g]) for g in range(n_parts)]
        for g in range(n_parts):
            y_ref[rows(g), :] = _layernorm(alpha * x1[g] + _dot(ctx[g], wo_ref[...]), ln2g_ref[...], ln2b_ref[...])

    assert tm % pass_rows == 0
    if tm == pass_rows:
        dense(0)
    else:
        def body(i, carry):
            dense(i)
            return carry
        lax.fori_loop(0, tm // pass_rows, body, 0)


def _mix_xattn(x4, fast, o2d, yr, yi, m2, wg, wab, wfb, wmix, ln1g, ln1b, wq, km, vm, wo, ln2g, ln2b, alpha):
    bsz, n, _, _ = x4.shape
    tm, tiles = n * fast, n // fast
    tok = lambda w: pl.BlockSpec((tm, w), lambda b, j: (b * tiles + j, 0))
    plane = pl.BlockSpec((None, FOURIER_GROUPS, fast, n, FOURIER_GROUP_W), lambda b, j: (b, 0, j, 0, 0))
    mem = pl.BlockSpec((MEM_TOKENS, D_MODEL), lambda b, j: (b, 0))
    mem_t = pl.BlockSpec((D_MODEL, MEM_TOKENS), lambda b, j: (b, 0))
    vec = _resident((1, D_MODEL))
    return pl.pallas_call(
        functools.partial(_mix_xattn_kernel, alpha=alpha),
        grid=(bsz, tiles),
        in_specs=[_tile_view(n, fast, D_MODEL), tok(ATTN_W), plane, plane, _resident((n, 2 * n)),
                  _resident((D_MODEL, 2 * D_MODEL)), _resident((ATTN_W, D_MODEL)),
                  _resident((FOURIER_W, D_MODEL)), _resident((D_MODEL, D_MODEL)), vec, vec,
                  _resident((D_MODEL, D_MODEL)), mem_t, mem, _resident((D_MODEL, D_MODEL)), vec, vec],
        out_specs=tok(D_MODEL),
        out_shape=jax.ShapeDtypeStruct((bsz * n * n, D_MODEL), F32),
        scratch_shapes=[pltpu.VMEM((FOURIER_GROUPS, tm, FOURIER_GROUP_W), F32)],
        compiler_params=_params(("parallel", "parallel")),
        name="mix_xattn",
    )(x4, o2d, yr, yi, m2, wg, wab, wfb, wmix, ln1g, ln1b, wq, km, vm, wo, ln2g, ln2b)


FF_CHUNK = 1024


def _mlp_kernel(x_ref, wup_ref, wdown_ref, g_ref, b_ref, y_ref, *, alpha):
    x = x_ref[...]
    xb = x.astype(BF16)
    y = alpha * x
    for c in range(0, D_FF, FF_CHUNK):
        h = jnp.maximum(_dot(xb, wup_ref[:, c:c + FF_CHUNK]), 0.0)
        y = y + _dot((h * h).astype(BF16), wdown_ref[c:c + FF_CHUNK, :])
    y_ref[...] = _layernorm(y, g_ref[...], b_ref[...]).reshape(y_ref.shape)


def _mlp(x2d, bsz, n, fast, wup, wdown, g, b, alpha):
    tm, tiles = n * fast, n // fast
    vec = _resident((1, D_MODEL))
    return pl.pallas_call(
        functools.partial(_mlp_kernel, alpha=alpha),
        grid=(bsz, tiles),
        in_specs=[pl.BlockSpec((tm, D_MODEL), lambda b, j: (b * tiles + j, 0)),
                  _resident((D_MODEL, D_FF)), _resident((D_FF, D_MODEL)), vec, vec],
        out_specs=_tile_view(n, fast, D_MODEL),
        out_shape=jax.ShapeDtypeStruct((bsz, n, n, D_MODEL), F32),
        compiler_params=_params(("parallel", "parallel")),
        name="mlp",
    )(x2d, wup, wdown, g, b)


def _tile(n, want):
    t = min(n, want)
    assert n % t == 0
    return t


def _layer(x, mem, w, alpha):
    bsz, seq_len, _ = x.shape
    n, fast = _grid_side(seq_len)
    x4 = x.reshape(bsz, n, n, D_MODEL)
    m1, m2 = _dft_matrices(n)
    qt, k, vt, yr, yi = _in_projection(x4, fast, w["qkvu"], w["qk_gain"], m1)
    tq = _tile(seq_len, 512)
    o = _attention(w["score_bound"], qt, k.reshape(bsz, seq_len, KV_W), vt, tq, _tile(seq_len, 512),
                   unroll=4 if seq_len % 8192 == 0 else 2, tiles_per_step=_tile(seq_len // tq, 16))
    km, vm = _mem_kv(mem.reshape(bsz * MEM_TOKENS, D_MODEL), w["mem_kv"])
    x2 = _mix_xattn(x4, fast, o.reshape(bsz * seq_len, ATTN_W), yr, yi, m2, w["gate"], w["attn_branch"],
                    w["fourier_branch"], w["mix_out"], w["ln1_g"], w["ln1_b"], w["mem_q"], km, vm,
                    w["mem_o"], w["ln2_g"], w["ln2_b"], alpha)
    y = _mlp(x2, bsz, n, fast, w["up"], w["down"], w["ln3_g"], w["ln3_b"], alpha)
    return y.reshape(bsz, seq_len, D_MODEL)


def kernel(x_prompt, x_sample, mem_prompt, mem_sample, w_in, q_norm, k_norm, w_attn_branch, w_fourier_branch, w_mix_out, ln1_g, ln1_b, w_mem_q, w_mem_k, w_mem_v, w_mem_o, ln2_g, ln2_b, w_up, w_down, ln3_g, ln3_b):
    depth = w_in.shape[0]
    alpha = float((2 * depth) ** 0.25)
    y_prompt, y_sample = x_prompt, x_sample
    for l in range(depth):
        row = lambda v: v[l].reshape(1, -1).astype(F32)
        w = {
            "qkvu": w_in[l, :, :QKVU_W].astype(BF16),
            "gate": w_in[l, :, QKVU_W:].astype(BF16),
            "qk_gain": jnp.concatenate([jnp.tile(q_norm[l] * (HEAD_DIM ** -0.5 * LOG2E), N_HEADS),
                                        jnp.tile(k_norm[l], N_KV_HEADS)]).reshape(1, -1).astype(F32),
            "score_bound": (HEAD_DIM ** 0.5 * LOG2E * BOUND_SLACK * jnp.max(jnp.abs(q_norm[l]))
                            * jnp.max(jnp.abs(k_norm[l]))).reshape(1).astype(F32),
            "attn_branch": w_attn_branch[l].astype(BF16),
            "fourier_branch": w_fourier_branch[l].astype(BF16),
            "mix_out": w_mix_out[l].astype(BF16),
            "ln1_g": row(ln1_g), "ln1_b": row(ln1_b),
            "mem_q": w_mem_q[l].astype(BF16),
            "mem_kv": jnp.concatenate([w_mem_k[l], w_mem_v[l]], axis=1).astype(BF16),
            "mem_o": w_mem_o[l].astype(BF16),
            "ln2_g": row(ln2_g), "ln2_b": row(ln2_b),
            "up": w_up[l].astype(BF16), "down": w_down[l].astype(BF16),
            "ln3_g": row(ln3_g), "ln3_b": row(ln3_b),
        }
        y_prompt = _layer(y_prompt, mem_prompt, w, alpha)
        y_sample = _layer(y_sample, mem_sample, w, alpha)
    return (y_prompt, y_sample)
```

```python
import functools
import math

import numpy as np
import jax
import jax.numpy as jnp
from jax import lax
from jax.experimental import pallas as pl
from jax.experimental.pallas import tpu as pltpu

D_MODEL = 1024
GRID_W = 64
N_HEADS = 8
N_KV_HEADS = 2
HEAD_DIM = 64
GROUP = N_HEADS // N_KV_HEADS
ATTN_W = N_HEADS * HEAD_DIM
KV_W = N_KV_HEADS * HEAD_DIM
ROPE_THETA = 10000.0
AXIS_ROT = HEAD_DIM // 2
FOURIER_GROUPS = 4
FOURIER_GROUP_W = 128
FOURIER_W = FOURIER_GROUPS * FOURIER_GROUP_W
QKVU_W = ATTN_W + 2 * KV_W + FOURIER_W
MEM_TOKENS = 256
MEM_HEADS = 4
MEM_HEAD_DIM = D_MODEL // MEM_HEADS
D_FF = 4 * D_MODEL
RMS_EPS = 1e-6
LN_EPS = 1e-5
LOG2E = math.log2(math.e)

V7X_VMEM_BYTES = 64 * 1024 * 1024
VMEM_LIMIT_BYTES = V7X_VMEM_BYTES - 8 * 1024 * 1024
LANES = 128
SUBLANES = 8

BF16 = jnp.bfloat16
F32 = jnp.float32


def _dot(a, b):
    return jnp.dot(a, b, preferred_element_type=F32)


def _resident(shape):
    n = len(shape)
    return pl.BlockSpec(shape, lambda *_: (0,) * n, pipeline_mode=pl.Buffered(1))


def _params(semantics):
    return pltpu.CompilerParams(dimension_semantics=semantics, vmem_limit_bytes=VMEM_LIMIT_BYTES)


def _layernorm(x, g, b):
    mu = jnp.mean(x, axis=-1, keepdims=True)
    xc = x - mu
    var = jnp.mean(xc * xc, axis=-1, keepdims=True)
    return xc * lax.rsqrt(var + LN_EPS) * g + b


TILE_TOKENS = 512


def _grid_side(seq_len):
    n = math.isqrt(seq_len)
    fast = max(SUBLANES, TILE_TOKENS // n)
    assert n * n == seq_len and fast % SUBLANES == 0 and n % fast == 0, "unsupported sequence length"
    return n, fast


def _tile_view(n, fast, width):
    return pl.BlockSpec((None, n, fast, width), lambda b, j: (b, 0, j, 0))


GATE_ROWS = 256

def _inproj_kernel(x_ref, w_ref, wg_ref, gain_ref, cos_ref, sin_ref, seg_ref, cdft_ref, m1_ref, twc_ref, tws_ref,
                   qt_ref, k_ref, vt_ref, yr_ref, yi_ref, gate_ref, zc_scr, zs_scr, yr_scr, yi_scr):
    n, fast = x_ref.shape[0], x_ref.shape[1]
    tm = n * fast
    xb = x_ref[...].reshape(tm, D_MODEL).astype(BF16)
    h = _dot(xb, w_ref[...])
    qk_w = ATTN_W + KV_W
    qk = h[:, :qk_w]
    sq = (qk * qk).astype(BF16)
    seg = seg_ref[...]
    ms = jnp.concatenate(
        [_dot(sq[:, c:c + 2 * LANES], seg) for c in range(0, ATTN_W, 2 * LANES)]
        + [_dot(sq[:, ATTN_W:qk_w], seg[:KV_W, :KV_W])], axis=-1)
    qkn = qk * lax.rsqrt(ms + RMS_EPS) * gain_ref[...]
    cos = cos_ref[...].reshape(tm, LANES)
    sin = sin_ref[...].reshape(tm, LANES)
    lane = lax.broadcasted_iota(jnp.int32, cos.shape, 1)
    first_half = (lane % HEAD_DIM) < (HEAD_DIM // 2)
    slabs = []
    for c in range(0, qk_w, LANES):
        xs = qkn[:, c:c + LANES]
        ahead = pltpu.roll(xs, LANES - HEAD_DIM // 2, 1)
        behind = pltpu.roll(xs, HEAD_DIM // 2, 1)
        slabs.append(xs * cos + jnp.where(first_half, ahead, behind) * sin)
    q = jnp.concatenate(slabs[:ATTN_W // LANES], axis=-1)
    qt_ref[...] = q.T.astype(BF16)
    k_ref[...] = slabs[-1].astype(BF16)
    vt_ref[...] = h[:, qk_w:qk_w + KV_W].T.astype(BF16)
    ub = h[:, qk_w + KV_W:].astype(BF16)
    cdft = cdft_ref[...]
    for gi in range(FOURIER_GROUPS):
        c = gi * FOURIER_GROUP_W
        z = _dot(ub[:, c:c + FOURIER_GROUP_W], cdft)
        zc_scr[gi] = z[:, :FOURIER_GROUP_W]
        zs_scr[gi] = z[:, FOURIER_GROUP_W:]
    m1 = m1_ref[...]
    reps = FOURIER_W // LANES

    def gate_rows(c):
        rows_c = slice(c * GATE_ROWS, (c + 1) * GATE_ROWS)
        gate_ref[rows_c, :] = jax.nn.sigmoid(_dot(xb[rows_c], wg_ref[...])).astype(BF16)

    n_gate = tm // GATE_ROWS
    gate_after = {(c * fast) // n_gate - 1: c for c in range(1, n_gate)}
    gate_rows(0)
    for r in range(fast):
        if r - 1 in gate_after:
            gate_rows(gate_after[r - 1])
        rows = pl.ds(r, n, stride=fast)
        gather = lambda scr: jnp.concatenate([scr.at[gi][rows, :] for gi in range(FOURIER_GROUPS)], axis=-1)
        x2 = jnp.concatenate([gather(zc_scr), gather(zs_scr)], axis=0).astype(BF16)
        y = _dot(m1, x2)
        yr, yi = y[:n], y[n:]
        c = jnp.concatenate([twc_ref[r]] * reps, axis=-1)
        s = jnp.concatenate([tws_ref[r]] * reps, axis=-1)
        for scr, val in ((yr_scr, yr * c + yi * s), (yi_scr, yi * c - yr * s)):
            for gi in range(FOURIER_GROUPS):
                scr.at[gi][rows, :] = val[:, gi * FOURIER_GROUP_W:(gi + 1) * FOURIER_GROUP_W]
    for gi in range(FOURIER_GROUPS):
        yr_ref[gi] = yr_scr[gi].reshape(n, fast, FOURIER_GROUP_W)
        yi_ref[gi] = yi_scr[gi].reshape(n, fast, FOURIER_GROUP_W)


def _rope_tables(seq_len):
    rows = seq_len // GRID_W
    row = jnp.repeat(jnp.arange(rows, dtype=F32), GRID_W)
    col = jnp.tile(jnp.arange(GRID_W, dtype=F32), rows)
    freqs = ROPE_THETA ** (-jnp.arange(0, AXIS_ROT, 2, dtype=F32) / AXIS_ROT)
    ang = jnp.concatenate([row[:, None] * freqs, col[:, None] * freqs], axis=-1)
    ang = jnp.concatenate([ang, ang], axis=-1)
    cos, sin = jnp.cos(ang), jnp.sin(ang)
    sign = jnp.where(jnp.arange(HEAD_DIM) < HEAD_DIM // 2, -1.0, 1.0).astype(F32)
    reps = LANES // HEAD_DIM
    return jnp.tile(cos, (1, reps)), jnp.tile(sin * sign, (1, reps))


def _segment_mean_matrix():
    idx = np.arange(2 * LANES)
    m = (idx[:, None] // HEAD_DIM == idx[None, :] // HEAD_DIM).astype(np.float32) / HEAD_DIM
    return jnp.asarray(m, dtype=BF16)


def _channel_dft_matrix():
    n = FOURIER_GROUP_W
    jk = np.outer(np.arange(n), np.arange(n)) % n
    ang = 2.0 * np.pi * jk / n
    m = np.concatenate([np.cos(ang), np.sin(ang)], axis=1) / math.sqrt(n)
    return jnp.asarray(m, dtype=BF16)


def _dft_matrices(n):
    ang = 2.0 * np.pi * (np.outer(np.arange(n), np.arange(n)) % n) / n
    c, s = np.cos(ang) / math.sqrt(n), np.sin(ang) / math.sqrt(n)
    m1 = np.block([[c, -s], [-s, -c]])
    m2 = np.concatenate([c, s], axis=1)
    return jnp.asarray(m1, dtype=BF16), jnp.asarray(m2, dtype=BF16)


def _twiddles(n):
    seq_len = n * n
    idx = jnp.arange(n, dtype=jnp.int32)
    ang = ((idx[:, None] * idx[None, :]) % seq_len).astype(F32) * (2.0 * math.pi / seq_len)
    rep = lambda t: jnp.broadcast_to(t[:, :, None], (n, n, LANES))
    return rep(jnp.cos(ang)), rep(jnp.sin(ang))


def _in_projection(x4, fast, w_qkvu, w_gate, gain, m1):
    bsz, n, _, _ = x4.shape
    seq_len, tm, tiles = n * n, n * fast, n // fast
    cos, sin = _rope_tables(seq_len)
    twc, tws = _twiddles(n)
    pos = pl.BlockSpec((n, fast, LANES), lambda b, j: (0, j, 0))
    tw = pl.BlockSpec((fast, n, LANES), lambda b, j: (j, 0, 0))
    tr = lambda rows: pl.BlockSpec((None, rows, tm), lambda b, j: (b, 0, j))
    plane = pl.BlockSpec((None, FOURIER_GROUPS, n, fast, FOURIER_GROUP_W), lambda b, j: (b, 0, 0, j, 0))
    return pl.pallas_call(
        _inproj_kernel,
        grid=(bsz, tiles),
        in_specs=[
            _tile_view(n, fast, D_MODEL),
            _resident((D_MODEL, QKVU_W)),
            _resident((D_MODEL, 2 * D_MODEL)),
            _resident((1, ATTN_W + KV_W)),
            pos, pos,
            _resident((2 * LANES, 2 * LANES)),
            _resident((FOURIER_GROUP_W, 2 * FOURIER_GROUP_W)),
            _resident((2 * n, 2 * n)),
            tw, tw,
        ],
        out_specs=[
            tr(ATTN_W),
            pl.BlockSpec((tm, KV_W), lambda b, j: (b * tiles + j, 0)),
            tr(KV_W),
            plane, plane,
            pl.BlockSpec((tm, 2 * D_MODEL), lambda b, j: (b * tiles + j, 0)),
        ],
        out_shape=[
            jax.ShapeDtypeStruct((bsz, ATTN_W, seq_len), BF16),
            jax.ShapeDtypeStruct((bsz * seq_len, KV_W), BF16),
            jax.ShapeDtypeStruct((bsz, KV_W, seq_len), BF16),
            jax.ShapeDtypeStruct((bsz, FOURIER_GROUPS, n, n, FOURIER_GROUP_W), F32),
            jax.ShapeDtypeStruct((bsz, FOURIER_GROUPS, n, n, FOURIER_GROUP_W), F32),
            jax.ShapeDtypeStruct((bsz * seq_len, 2 * D_MODEL), BF16),
        ],
        scratch_shapes=[pltpu.VMEM((FOURIER_GROUPS, tm, FOURIER_GROUP_W), F32)] * 4,
        compiler_params=_params(("parallel", "parallel")),
        name="in_projection",
    )(x4, w_qkvu, w_gate, gain, cos.reshape(n, n, LANES), sin.reshape(n, n, LANES),
      _segment_mean_matrix(), _channel_dft_matrix(), m1, twc, tws)


SUM_ROWS = 16
KEY_SUBTILE = 256
SAFE_SCORE_BOUND = 60.0
BOUND_SLACK = 1.05
BOUNDED_GROUP_KEYS = 8192


def _attention_kernel(bound_ref, qt_ref, k_ref, vt_ref, o_ref, qpad_ref, sa_ref, sb_ref, acc_ref, *,
                      tq, tk, unroll):
    n_tiles = qt_ref.shape[-1] // tq
    seq_len = k_ref.shape[0]
    n_chunks = seq_len // tk
    assert unroll % 2 == 0 and n_chunks % unroll == 0 and tk % KEY_SUBTILE == 0
    kv_head = pl.program_id(1)
    row_head = lax.broadcasted_iota(jnp.int32, (KV_W, tq), 0) // HEAD_DIM
    ones = jnp.ones((SUM_ROWS, tk), BF16)
    bufs = (sa_ref, sb_ref)

    def load_queries(tile, slot):
        cols = pl.ds(pl.multiple_of(tile * tq, tq), tq)
        for h in range(GROUP):
            qh = qt_ref[h * HEAD_DIM:(h + 1) * HEAD_DIM, cols]
            q_rep = jnp.concatenate([qh] * N_KV_HEADS, axis=0)
            qpad_ref[slot, h] = jnp.where(row_head == kv_head, q_rep, jnp.zeros_like(q_rep))

    def key_chunk(c):
        return pl.ds(pl.multiple_of(c * tk, tk), tk)

    def step(q_slot, c_next, s_next_ref, c_cur, s_cur_ref, m_run, m_chunk):
        kc = k_ref[key_chunk(c_next), :]
        if c_cur is not None:
            vc = jnp.concatenate([vt_ref[:, key_chunk(c_cur)], ones], axis=0)
        m_next, maxima = [], []
        for h in range(GROUP):
            if c_cur is not None:
                m_new = jnp.maximum(m_run[h], m_chunk[h])
                alpha = jnp.exp2(m_run[h] - m_new)
                m_next.append(m_new)
            pv, mx = None, None
            for t in range(0, tk, KEY_SUBTILE):
                s = _dot(kc[t:t + KEY_SUBTILE], qpad_ref[q_slot, h])
                s_next_ref[h, t:t + KEY_SUBTILE, :] = s
                smax = jnp.max(s, axis=0, keepdims=True)
                mx = smax if mx is None else jnp.maximum(mx, smax)
                if c_cur is not None:
                    p = jnp.exp2(s_cur_ref[h, t:t + KEY_SUBTILE, :] - m_new).astype(BF16)
                    d = _dot(vc[:, t:t + KEY_SUBTILE], p)
                    pv = d if pv is None else pv + d
            if c_cur is not None:
                acc_ref[h] = alpha * acc_ref[h] + pv
            maxima.append(mx)
        return tuple(m_next), tuple(maxima)

    def tile_body(tile, m_chunk):
        slot = tile % 2
        load_queries(jnp.minimum(tile + 1, n_tiles - 1), 1 - slot)
        acc_ref[...] = jnp.zeros_like(acc_ref)
        m_run = tuple(jnp.full((1, tq), -jnp.inf, F32) for _ in range(GROUP))

        def group(j, carry):
            m_run, m_chunk = carry
            for u in range(unroll):
                c = unroll * j + u
                m_run, m_chunk = step(slot, c + 1, bufs[(u + 1) % 2], c, bufs[u % 2], m_run, m_chunk)
            return m_run, m_chunk

        m_run, m_chunk = lax.fori_loop(0, n_chunks // unroll - 1, group, (m_run, m_chunk))
        for c in range(n_chunks - unroll, n_chunks - 1):
            m_run, m_chunk = step(slot, c + 1, bufs[(c + 1) % 2], c, bufs[c % 2], m_run, m_chunk)
        _, m_chunk = step(1 - slot, 0, bufs[0], n_chunks - 1, bufs[(n_chunks - 1) % 2], m_run, m_chunk)
        write_tile(tile, partial_sums=False)
        return m_chunk

    def write_tile(tile, partial_sums):
        outs = []
        for h in range(GROUP):
            a = acc_ref[h]
            total = (jnp.sum(a[HEAD_DIM:HEAD_DIM + SUBLANES], axis=0, keepdims=True) if partial_sums
                     else a[HEAD_DIM:HEAD_DIM + 1])
            outs.append(a[:HEAD_DIM] / total)
        rows = pl.ds(pl.multiple_of(tile * tq, tq), tq)
        o_ref[rows, :] = jnp.concatenate(outs, axis=0).T.astype(BF16)

    group_keys = min(seq_len, BOUNDED_GROUP_KEYS)
    n_sub = group_keys // KEY_SUBTILE

    def bounded_tile(tile, carry):
        load_queries(tile, 0)
        acc_ref[...] = jnp.zeros_like(acc_ref)

        def group(j, carry):
            def keys(t):
                return pl.ds(pl.multiple_of(j * group_keys + t * KEY_SUBTILE, KEY_SUBTILE), KEY_SUBTILE)
            pv = [None] * GROUP
            psum = [None] * GROUP
            s_prev = [None] * GROUP
            for t in range(n_sub + 1):
                if t < n_sub:
                    kt = k_ref[keys(t), :]
                if t > 0:
                    vt = vt_ref[:, keys(t - 1)]
                for h in range(GROUP):
                    s_cur = _dot(kt, qpad_ref[0, h]) if t < n_sub else None
                    if t > 0:
                        p = jnp.exp2(s_prev[h])
                        d = _dot(vt, p.astype(BF16))
                        ps = jnp.sum(p.reshape(KEY_SUBTILE // SUBLANES, SUBLANES, tq), axis=0)
                        pv[h] = d if pv[h] is None else pv[h] + d
                        psum[h] = ps if psum[h] is None else psum[h] + ps
                    s_prev[h] = s_cur
            for h in range(GROUP):
                acc_ref[h, :HEAD_DIM, :] += pv[h]
                acc_ref[h, HEAD_DIM:HEAD_DIM + SUBLANES, :] += psum[h]
            return carry

        lax.fori_loop(0, seq_len // group_keys, group, 0)
        write_tile(tile, partial_sums=True)
        return carry

    bounded = bound_ref[0] <= SAFE_SCORE_BOUND

    @pl.when(bounded)
    def _():
        lax.fori_loop(0, n_tiles, bounded_tile, 0)

    @pl.when(jnp.logical_not(bounded))
    def _():
        load_queries(0, 0)
        _, m_chunk = step(0, 0, bufs[0], None, None, None, None)
        lax.fori_loop(0, n_tiles, tile_body, m_chunk)


def _attention(score_bound, qt, k, vt, tq, tk, unroll, tiles_per_step):
    bsz, _, seq_len = qt.shape
    gw = GROUP * HEAD_DIM
    tqb = tq * tiles_per_step
    assert seq_len % tqb == 0
    return pl.pallas_call(
        functools.partial(_attention_kernel, tq=tq, tk=tk, unroll=unroll),
        grid=(bsz, N_KV_HEADS, seq_len // tqb),
        in_specs=[
            pl.BlockSpec(memory_space=pltpu.SMEM),
            pl.BlockSpec((None, gw, tqb), lambda b, g, i: (b, g, i)),
            pl.BlockSpec((None, seq_len, KV_W), lambda b, g, i: (b, 0, 0)),
            pl.BlockSpec((None, HEAD_DIM, seq_len), lambda b, g, i: (b, g, 0)),
        ],
        out_specs=pl.BlockSpec((None, tqb, gw), lambda b, g, i: (b, i, g)),
        out_shape=jax.ShapeDtypeStruct((bsz, seq_len, ATTN_W), BF16),
        scratch_shapes=[pltpu.VMEM((2, GROUP, KV_W, tq), BF16),
                        pltpu.VMEM((GROUP, tk, tq), F32),
                        pltpu.VMEM((GROUP, tk, tq), F32),
                        pltpu.VMEM((GROUP, HEAD_DIM + SUM_ROWS, tq), F32)],
        compiler_params=_params(("parallel", "parallel", "parallel")),
        name="gqa_attention",
    )(score_bound, qt, k, vt)


def _mem_kv_kernel(mem_ref, w_ref, kt_ref, v_ref):
    y = _dot(mem_ref[...].astype(BF16), w_ref[...])
    kt_ref[...] = y[:, :D_MODEL].T.astype(BF16)
    v_ref[...] = y[:, D_MODEL:].astype(BF16)


def _mem_kv(mem2d, w_kv):
    rows = mem2d.shape[0]
    blk = pl.BlockSpec((MEM_TOKENS, D_MODEL), lambda i: (i, 0))
    blk_t = pl.BlockSpec((D_MODEL, MEM_TOKENS), lambda i: (i, 0))
    return pl.pallas_call(
        _mem_kv_kernel,
        grid=(rows // MEM_TOKENS,),
        in_specs=[blk, _resident((D_MODEL, 2 * D_MODEL))],
        out_specs=[blk_t, blk],
        out_shape=[jax.ShapeDtypeStruct((rows // MEM_TOKENS * D_MODEL, MEM_TOKENS), BF16),
                   jax.ShapeDtypeStruct((rows, D_MODEL), BF16)],
        compiler_params=_params(("parallel",)),
        name="mem_kv",
    )(mem2d, w_kv)


MIX_ROWS = 512
MIX_PARTS = 2


def _mix_xattn_kernel(x_ref, o_ref, gate_ref, yr_ref, yi_ref, m2_ref, wab_ref, wfb_ref, wmix_ref,
                      ln1g_ref, ln1b_ref, wq_ref, kmt_ref, vm_ref, wo_ref, ln2g_ref, ln2b_ref,
                      y_ref, f_scr, *, alpha):
    n, fast = x_ref.shape[0], x_ref.shape[1]
    tm = n * fast
    m2 = m2_ref[...]
    for r in range(fast):
        plane = lambda ref: jnp.concatenate([ref[gi, r] for gi in range(FOURIER_GROUPS)], axis=-1)
        y2 = jnp.concatenate([plane(yr_ref), plane(yi_ref)], axis=0).astype(BF16)
        f = _dot(m2, y2)
        for gi in range(FOURIER_GROUPS):
            f_scr.at[gi][pl.ds(r, n, stride=fast), :] = f[:, gi * FOURIER_GROUP_W:(gi + 1) * FOURIER_GROUP_W]

    pass_rows = min(MIX_ROWS, tm)

    n_parts = MIX_PARTS if pass_rows % (MIX_PARTS * fast * SUBLANES) == 0 else 1
    part = pass_rows // n_parts

    def dense(i):
        def rows(g):
            return pl.ds(pl.multiple_of(i * pass_rows + g * part, part), part)

        def merge(g):
            slow = pl.ds(pl.multiple_of((i * pass_rows + g * part) // fast, part // fast), part // fast)
            x = x_ref[slow].reshape(part, D_MODEL)
            gates = gate_ref[rows(g), :].astype(F32)
            y_attn = _dot(o_ref[rows(g), :], wab_ref[...])
            f = jnp.concatenate([f_scr[gi, rows(g), :] for gi in range(FOURIER_GROUPS)], axis=-1)
            y_four = _dot(f.astype(BF16), wfb_ref[...])
            merged = gates[:, :D_MODEL] * y_attn + gates[:, D_MODEL:] * y_four
            return _layernorm(alpha * x + _dot(merged.astype(BF16), wmix_ref[...]), ln1g_ref[...], ln1b_ref[...])

        def cross_attend(x1):
            q = (_dot(x1.astype(BF16), wq_ref[...]) * (MEM_HEAD_DIM ** -0.5 * LOG2E)).astype(BF16)
            heads = []
            for h in range(MEM_HEADS):
                c = h * MEM_HEAD_DIM
                s = _dot(q[:, c:c + MEM_HEAD_DIM], kmt_ref[c:c + MEM_HEAD_DIM, :])
                p = jnp.exp2(s - jnp.max(s, axis=-1, keepdims=True))
                l = jnp.sum(p, axis=-1, keepdims=True)
                heads.append(_dot(p.astype(BF16), vm_ref[:, c:c + MEM_HEAD_DIM]) / l)
            return jnp.concatenate(heads, axis=-1).astype(BF16)

        x1 = [merge(g) for g in range(n_parts)]
        ctx = [cross_attend(x1[g]) for g in range(n_parts)]
        for g in range(n_parts):
            y_ref[rows(g), :] = _layernorm(alpha * x1[g] + _dot(ctx[g], wo_ref[...]), ln2g_ref[...], ln2b_ref[...])

    assert tm % pass_rows == 0
    if tm == pass_rows:
        dense(0)
    else:
        def body(i, carry):
            dense(i)
            return carry
        lax.fori_loop(0, tm // pass_rows, body, 0)


def _mix_xattn(x4, fast, o2d, gates, yr, yi, m2, wab, wfb, wmix, ln1g, ln1b, wq, km, vm, wo, ln2g, ln2b, alpha):
    bsz, n, _, _ = x4.shape
    tm, tiles = n * fast, n // fast
    tok = lambda w: pl.BlockSpec((tm, w), lambda b, j: (b * tiles + j, 0))
    plane = pl.BlockSpec((None, FOURIER_GROUPS, fast, n, FOURIER_GROUP_W), lambda b, j: (b, 0, j, 0, 0))
    mem = pl.BlockSpec((MEM_TOKENS, D_MODEL), lambda b, j: (b, 0))
    mem_t = pl.BlockSpec((D_MODEL, MEM_TOKENS), lambda b, j: (b, 0))
    vec = _resident((1, D_MODEL))
    return pl.pallas_call(
        functools.partial(_mix_xattn_kernel, alpha=alpha),
        grid=(bsz, tiles),
        in_specs=[_tile_view(n, fast, D_MODEL), tok(ATTN_W), tok(2 * D_MODEL), plane, plane, _resident((n, 2 * n)),
                  _resident((ATTN_W, D_MODEL)),
                  _resident((FOURIER_W, D_MODEL)), _resident((D_MODEL, D_MODEL)), vec, vec,
                  _resident((D_MODEL, D_MODEL)), mem_t, mem, _resident((D_MODEL, D_MODEL)), vec, vec],
        out_specs=tok(D_MODEL),
        out_shape=jax.ShapeDtypeStruct((bsz * n * n, D_MODEL), F32),
        scratch_shapes=[pltpu.VMEM((FOURIER_GROUPS, tm, FOURIER_GROUP_W), F32)],
        compiler_params=_params(("parallel", "parallel")),
        name="mix_xattn",
    )(x4, o2d, gates, yr, yi, m2, wab, wfb, wmix, ln1g, ln1b, wq, km, vm, wo, ln2g, ln2b)


FF_CHUNK = 1024


def _mlp_kernel(x_ref, wup_ref, wdown_ref, g_ref, b_ref, y_ref, *, alpha):
    x = x_ref[...]
    xb = x.astype(BF16)
    y = alpha * x
    for c in range(0, D_FF, FF_CHUNK):
        h = jnp.maximum(_dot(xb, wup_ref[:, c:c + FF_CHUNK]), 0.0)
        y = y + _dot((h * h).astype(BF16), wdown_ref[c:c + FF_CHUNK, :])
    y_ref[...] = _layernorm(y, g_ref[...], b_ref[...]).reshape(y_ref.shape)


def _mlp(x2d, bsz, n, fast, wup, wdown, g, b, alpha):
    tm, tiles = n * fast, n // fast
    vec = _resident((1, D_MODEL))
    return pl.pallas_call(
        functools.partial(_mlp_kernel, alpha=alpha),
        grid=(bsz, tiles),
        in_specs=[pl.BlockSpec((tm, D_MODEL), lambda b, j: (b * tiles + j, 0)),
                  _resident((D_MODEL, D_FF)), _resident((D_FF, D_MODEL)), vec, vec],
        out_specs=_tile_view(n, fast, D_MODEL),
        out_shape=jax.ShapeDtypeStruct((bsz, n, n, D_MODEL), F32),
        compiler_params=_params(("parallel", "parallel")),
        name="mlp",
    )(x2d, wup, wdown, g, b)


def _tile(n, want):
    t = min(n, want)
    assert n % t == 0
    return t


def _layer(x, mem, w, alpha):
    bsz, seq_len, _ = x.shape
    n, fast = _grid_side(seq_len)
    x4 = x.reshape(bsz, n, n, D_MODEL)
    m1, m2 = _dft_matrices(n)
    qt, k, vt, yr, yi, gates = _in_projection(x4, fast, w["qkvu"], w["gate"], w["qk_gain"], m1)
    tq = _tile(seq_len, 512)
    o = _attention(w["score_bound"], qt, k.reshape(bsz, seq_len, KV_W), vt, tq, _tile(seq_len, 512),
                   unroll=4 if seq_len % 8192 == 0 else 2, tiles_per_step=_tile(seq_len // tq, 16))
    km, vm = _mem_kv(mem.reshape(bsz * MEM_TOKENS, D_MODEL), w["mem_kv"])
    x2 = _mix_xattn(x4, fast, o.reshape(bsz * seq_len, ATTN_W), gates, yr, yi, m2, w["attn_branch"],
                    w["fourier_branch"], w["mix_out"], w["ln1_g"], w["ln1_b"], w["mem_q"], km, vm,
                    w["mem_o"], w["ln2_g"], w["ln2_b"], alpha)
    y = _mlp(x2, bsz, n, fast, w["up"], w["down"], w["ln3_g"], w["ln3_b"], alpha)
    return y.reshape(bsz, seq_len, D_MODEL)


def kernel(x_prompt, x_sample, mem_prompt, mem_sample, w_in, q_norm, k_norm, w_attn_branch, w_fourier_branch, w_mix_out, ln1_g, ln1_b, w_mem_q, w_mem_k, w_mem_v, w_mem_o, ln2_g, ln2_b, w_up, w_down, ln3_g, ln3_b):
    depth = w_in.shape[0]
    alpha = float((2 * depth) ** 0.25)
    y_prompt, y_sample = x_prompt, x_sample
    for l in range(depth):
        row = lambda v: v[l].reshape(1, -1).astype(F32)
        w = {
            "qkvu": w_in[l, :, :QKVU_W].astype(BF16),
            "gate": w_in[l, :, QKVU_W:].astype(BF16),
            "qk_gain": jnp.concatenate([jnp.tile(q_norm[l] * (HEAD_DIM ** -0.5 * LOG2E), N_HEADS),
                                        jnp.tile(k_norm[l], N_KV_HEADS)]).reshape(1, -1).astype(F32),
            "score_bound": (HEAD_DIM ** 0.5 * LOG2E * BOUND_SLACK * jnp.max(jnp.abs(q_norm[l]))
                            * jnp.max(jnp.abs(k_norm[l]))).reshape(1).astype(F32),
            "attn_branch": w_attn_branch[l].astype(BF16),
            "fourier_branch": w_fourier_branch[l].astype(BF16),
            "mix_out": w_mix_out[l].astype(BF16),
            "ln1_g": row(ln1_g), "ln1_b": row(ln1_b),
            "mem_q": w_mem_q[l].astype(BF16),
            "mem_kv": jnp.concatenate([w_mem_k[l], w_mem_v[l]], axis=1).astype(BF16),
            "mem_o": w_mem_o[l].astype(BF16),
            "ln2_g": row(ln2_g), "ln2_b": row(ln2_b),
            "up": w_up[l].astype(BF16), "down": w_down[l].astype(BF16),
            "ln3_g": row(ln3_g), "ln3_b": row(ln3_b),
        }
        y_prompt = _layer(y_prompt, mem_prompt, w, alpha)
        y_sample = _layer(y_sample, mem_sample, w, alpha)
    return (y_prompt, y_sample)
```

```python
import functools
import math

import numpy as np
import jax
import jax.numpy as jnp
from jax import lax
from jax.experimental import pallas as pl
from jax.experimental.pallas import tpu as pltpu

D_MODEL = 1024
GRID_W = 64
N_HEADS = 8
N_KV_HEADS = 2
HEAD_DIM = 64
GROUP = N_HEADS // N_KV_HEADS
ATTN_W = N_HEADS * HEAD_DIM
KV_W = N_KV_HEADS * HEAD_DIM
ROPE_THETA = 10000.0
AXIS_ROT = HEAD_DIM // 2
FOURIER_GROUPS = 4
FOURIER_GROUP_W = 128
FOURIER_W = FOURIER_GROUPS * FOURIER_GROUP_W
QKVU_W = ATTN_W + 2 * KV_W + FOURIER_W
MEM_TOKENS = 256
MEM_HEADS = 4
MEM_HEAD_DIM = D_MODEL // MEM_HEADS
D_FF = 4 * D_MODEL
RMS_EPS = 1e-6
LN_EPS = 1e-5
LOG2E = math.log2(math.e)

V7X_VMEM_BYTES = 64 * 1024 * 1024
VMEM_LIMIT_BYTES = V7X_VMEM_BYTES - 12 * 1024 * 1024
LANES = 128
SUBLANES = 8

BF16 = jnp.bfloat16
F32 = jnp.float32


def _dot(a, b):
    return jnp.dot(a, b, preferred_element_type=F32)


def _resident(shape):
    n = len(shape)
    return pl.BlockSpec(shape, lambda *_: (0,) * n, pipeline_mode=pl.Buffered(1))


def _params(semantics):
    return pltpu.CompilerParams(dimension_semantics=semantics, vmem_limit_bytes=VMEM_LIMIT_BYTES)


def _layernorm(x, g, b):
    mu = jnp.mean(x, axis=-1, keepdims=True)
    xc = x - mu
    var = jnp.mean(xc * xc, axis=-1, keepdims=True)
    return xc * lax.rsqrt(var + LN_EPS) * g + b


TILE_TOKENS = 512


def _grid_side(seq_len):
    n = math.isqrt(seq_len)
    fast = max(SUBLANES, TILE_TOKENS // n)
    assert n * n == seq_len and fast % SUBLANES == 0 and n % fast == 0, "unsupported sequence length"
    return n, fast


def _tile_view(n, fast, width):
    return pl.BlockSpec((None, n, fast, width), lambda b, j: (b, 0, j, 0))


def _inproj_kernel(x_ref, w_ref, gain_ref, cos_ref, sin_ref, seg_ref, cdft_ref, m1_ref, twc_ref, tws_ref,
                   qt_ref, k_ref, vt_ref, yr_ref, yi_ref, zc_scr, zs_scr, yr_scr, yi_scr):
    n, fast = x_ref.shape[0], x_ref.shape[1]
    tm = n * fast
    xb = x_ref[...].reshape(tm, D_MODEL).astype(BF16)
    h = _dot(xb, w_ref[...])
    qk_w = ATTN_W + KV_W
    qk = h[:, :qk_w]
    sq = (qk * qk).astype(BF16)
    seg = seg_ref[...]
    ms = jnp.concatenate(
        [_dot(sq[:, c:c + 2 * LANES], seg) for c in range(0, ATTN_W, 2 * LANES)]
        + [_dot(sq[:, ATTN_W:qk_w], seg[:KV_W, :KV_W])], axis=-1)
    qkn = qk * lax.rsqrt(ms + RMS_EPS) * gain_ref[...]
    cos = cos_ref[...].reshape(tm, LANES)
    sin = sin_ref[...].reshape(tm, LANES)
    lane = lax.broadcasted_iota(jnp.int32, cos.shape, 1)
    first_half = (lane % HEAD_DIM) < (HEAD_DIM // 2)
    slabs = []
    for c in range(0, qk_w, LANES):
        xs = qkn[:, c:c + LANES]
        ahead = pltpu.roll(xs, LANES - HEAD_DIM // 2, 1)
        behind = pltpu.roll(xs, HEAD_DIM // 2, 1)
        slabs.append(xs * cos + jnp.where(first_half, ahead, behind) * sin)
    q = jnp.concatenate(slabs[:ATTN_W // LANES], axis=-1)
    qt_ref[...] = q.T.astype(BF16)
    k_ref[...] = slabs[-1].astype(BF16)
    vt_ref[...] = h[:, qk_w:qk_w + KV_W].T.astype(BF16)
    ub = h[:, qk_w + KV_W:].astype(BF16)
    cdft = cdft_ref[...]
    for gi in range(FOURIER_GROUPS):
        c = gi * FOURIER_GROUP_W
        z = _dot(ub[:, c:c + FOURIER_GROUP_W], cdft)
        zc_scr[gi] = z[:, :FOURIER_GROUP_W]
        zs_scr[gi] = z[:, FOURIER_GROUP_W:]
    m1 = m1_ref[...]
    reps = FOURIER_W // LANES
    for r in range(fast):
        rows = pl.ds(r, n, stride=fast)
        gather = lambda scr: jnp.concatenate([scr.at[gi][rows, :] for gi in range(FOURIER_GROUPS)], axis=-1)
        x2 = jnp.concatenate([gather(zc_scr), gather(zs_scr)], axis=0).astype(BF16)
        y = _dot(m1, x2)
        yr, yi = y[:n], y[n:]
        c = jnp.concatenate([twc_ref[r]] * reps, axis=-1)
        s = jnp.concatenate([tws_ref[r]] * reps, axis=-1)
        for scr, val in ((yr_scr, yr * c + yi * s), (yi_scr, yi * c - yr * s)):
            for gi in range(FOURIER_GROUPS):
                scr.at[gi][rows, :] = val[:, gi * FOURIER_GROUP_W:(gi + 1) * FOURIER_GROUP_W]
    for gi in range(FOURIER_GROUPS):
        yr_ref[gi] = yr_scr[gi].reshape(n, fast, FOURIER_GROUP_W)
        yi_ref[gi] = yi_scr[gi].reshape(n, fast, FOURIER_GROUP_W)


def _rope_tables(seq_len):
    rows = seq_len // GRID_W
    row = jnp.repeat(jnp.arange(rows, dtype=F32), GRID_W)
    col = jnp.tile(jnp.arange(GRID_W, dtype=F32), rows)
    freqs = ROPE_THETA ** (-jnp.arange(0, AXIS_ROT, 2, dtype=F32) / AXIS_ROT)
    ang = jnp.concatenate([row[:, None] * freqs, col[:, None] * freqs], axis=-1)
    ang = jnp.concatenate([ang, ang], axis=-1)
    cos, sin = jnp.cos(ang), jnp.sin(ang)
    sign = jnp.where(jnp.arange(HEAD_DIM) < HEAD_DIM // 2, -1.0, 1.0).astype(F32)
    reps = LANES // HEAD_DIM
    return jnp.tile(cos, (1, reps)), jnp.tile(sin * sign, (1, reps))


def _segment_mean_matrix():
    idx = np.arange(2 * LANES)
    m = (idx[:, None] // HEAD_DIM == idx[None, :] // HEAD_DIM).astype(np.float32) / HEAD_DIM
    return jnp.asarray(m, dtype=BF16)


def _channel_dft_matrix():
    n = FOURIER_GROUP_W
    jk = np.outer(np.arange(n), np.arange(n)) % n
    ang = 2.0 * np.pi * jk / n
    m = np.concatenate([np.cos(ang), np.sin(ang)], axis=1) / math.sqrt(n)
    return jnp.asarray(m, dtype=BF16)


def _dft_matrices(n):
    ang = 2.0 * np.pi * (np.outer(np.arange(n), np.arange(n)) % n) / n
    c, s = np.cos(ang) / math.sqrt(n), np.sin(ang) / math.sqrt(n)
    m1 = np.block([[c, -s], [-s, -c]])
    m2 = np.concatenate([c, s], axis=1)
    return jnp.asarray(m1, dtype=BF16), jnp.asarray(m2, dtype=BF16)


def _twiddles(n):
    seq_len = n * n
    idx = jnp.arange(n, dtype=jnp.int32)
    ang = ((idx[:, None] * idx[None, :]) % seq_len).astype(F32) * (2.0 * math.pi / seq_len)
    rep = lambda t: jnp.broadcast_to(t[:, :, None], (n, n, LANES))
    return rep(jnp.cos(ang)), rep(jnp.sin(ang))


def _in_projection(x4, fast, w_qkvu, gain, m1):
    bsz, n, _, _ = x4.shape
    seq_len, tm, tiles = n * n, n * fast, n // fast
    cos, sin = _rope_tables(seq_len)
    twc, tws = _twiddles(n)
    pos = pl.BlockSpec((n, fast, LANES), lambda b, j: (0, j, 0))
    tw = pl.BlockSpec((fast, n, LANES), lambda b, j: (j, 0, 0))
    tr = lambda rows: pl.BlockSpec((None, rows, tm), lambda b, j: (b, 0, j))
    plane = pl.BlockSpec((None, FOURIER_GROUPS, n, fast, FOURIER_GROUP_W), lambda b, j: (b, 0, 0, j, 0))
    return pl.pallas_call(
        _inproj_kernel,
        grid=(bsz, tiles),
        in_specs=[
            _tile_view(n, fast, D_MODEL),
            _resident((D_MODEL, QKVU_W)),
            _resident((1, ATTN_W + KV_W)),
            pos, pos,
            _resident((2 * LANES, 2 * LANES)),
            _resident((FOURIER_GROUP_W, 2 * FOURIER_GROUP_W)),
            _resident((2 * n, 2 * n)),
            tw, tw,
        ],
        out_specs=[
            tr(ATTN_W),
            pl.BlockSpec((tm, KV_W), lambda b, j: (b * tiles + j, 0)),
            tr(KV_W),
            plane, plane,
        ],
        out_shape=[
            jax.ShapeDtypeStruct((bsz, ATTN_W, seq_len), BF16),
            jax.ShapeDtypeStruct((bsz * seq_len, KV_W), BF16),
            jax.ShapeDtypeStruct((bsz, KV_W, seq_len), BF16),
            jax.ShapeDtypeStruct((bsz, FOURIER_GROUPS, n, n, FOURIER_GROUP_W), F32),
            jax.ShapeDtypeStruct((bsz, FOURIER_GROUPS, n, n, FOURIER_GROUP_W), F32),
        ],
        scratch_shapes=[pltpu.VMEM((FOURIER_GROUPS, tm, FOURIER_GROUP_W), F32)] * 4,
        compiler_params=_params(("parallel", "parallel")),
        name="in_projection",
    )(x4, w_qkvu, gain, cos.reshape(n, n, LANES), sin.reshape(n, n, LANES),
      _segment_mean_matrix(), _channel_dft_matrix(), m1, twc, tws)


SUM_ROWS = 16
KEY_SUBTILE = 256
SAFE_SCORE_BOUND = 60.0
BOUND_SLACK = 1.05
BOUNDED_GROUP_KEYS = 16384


def _attention_kernel(bound_ref, qt_ref, k_ref, vt_ref, o_ref, qpad_ref, sa_ref, sb_ref, acc_ref, *,
                      tq, tk, unroll):
    n_tiles = qt_ref.shape[-1] // tq
    seq_len = k_ref.shape[0]
    n_chunks = seq_len // tk
    assert unroll % 2 == 0 and n_chunks % unroll == 0 and tk % KEY_SUBTILE == 0
    kv_head = pl.program_id(1)
    row_head = lax.broadcasted_iota(jnp.int32, (KV_W, tq), 0) // HEAD_DIM
    ones = jnp.ones((SUM_ROWS, tk), BF16)
    bufs = (sa_ref, sb_ref)

    def load_queries(tile, slot):
        cols = pl.ds(pl.multiple_of(tile * tq, tq), tq)
        for h in range(GROUP):
            qh = qt_ref[h * HEAD_DIM:(h + 1) * HEAD_DIM, cols]
            q_rep = jnp.concatenate([qh] * N_KV_HEADS, axis=0)
            qpad_ref[slot, h] = jnp.where(row_head == kv_head, q_rep, jnp.zeros_like(q_rep))

    def key_chunk(c):
        return pl.ds(pl.multiple_of(c * tk, tk), tk)

    def step(q_slot, c_next, s_next_ref, c_cur, s_cur_ref, m_run, m_chunk):
        kc = k_ref[key_chunk(c_next), :]
        if c_cur is not None:
            vc = jnp.concatenate([vt_ref[:, key_chunk(c_cur)], ones], axis=0)
        m_next, maxima = [], []
        for h in range(GROUP):
            if c_cur is not None:
                m_new = jnp.maximum(m_run[h], m_chunk[h])
                alpha = jnp.exp2(m_run[h] - m_new)
                m_next.append(m_new)
            pv, mx = None, None
            for t in range(0, tk, KEY_SUBTILE):
                s = _dot(kc[t:t + KEY_SUBTILE], qpad_ref[q_slot, h])
                s_next_ref[h, t:t + KEY_SUBTILE, :] = s
                smax = jnp.max(s, axis=0, keepdims=True)
                mx = smax if mx is None else jnp.maximum(mx, smax)
                if c_cur is not None:
                    p = jnp.exp2(s_cur_ref[h, t:t + KEY_SUBTILE, :] - m_new).astype(BF16)
                    d = _dot(vc[:, t:t + KEY_SUBTILE], p)
                    pv = d if pv is None else pv + d
            if c_cur is not None:
                acc_ref[h] = alpha * acc_ref[h] + pv
            maxima.append(mx)
        return tuple(m_next), tuple(maxima)

    def tile_body(tile, m_chunk):
        slot = tile % 2
        load_queries(jnp.minimum(tile + 1, n_tiles - 1), 1 - slot)
        acc_ref[...] = jnp.zeros_like(acc_ref)
        m_run = tuple(jnp.full((1, tq), -jnp.inf, F32) for _ in range(GROUP))

        def group(j, carry):
            m_run, m_chunk = carry
            for u in range(unroll):
                c = unroll * j + u
                m_run, m_chunk = step(slot, c + 1, bufs[(u + 1) % 2], c, bufs[u % 2], m_run, m_chunk)
            return m_run, m_chunk

        m_run, m_chunk = lax.fori_loop(0, n_chunks // unroll - 1, group, (m_run, m_chunk))
        for c in range(n_chunks - unroll, n_chunks - 1):
            m_run, m_chunk = step(slot, c + 1, bufs[(c + 1) % 2], c, bufs[c % 2], m_run, m_chunk)
        _, m_chunk = step(1 - slot, 0, bufs[0], n_chunks - 1, bufs[(n_chunks - 1) % 2], m_run, m_chunk)
        write_tile(tile, partial_sums=False)
        return m_chunk

    def write_tile(tile, partial_sums):
        outs = []
        for h in range(GROUP):
            a = acc_ref[h]
            total = (jnp.sum(a[HEAD_DIM:HEAD_DIM + SUBLANES], axis=0, keepdims=True) if partial_sums
                     else a[HEAD_DIM:HEAD_DIM + 1])
            outs.append(a[:HEAD_DIM] / total)
        rows = pl.ds(pl.multiple_of(tile * tq, tq), tq)
        o_ref[rows, :] = jnp.concatenate(outs, axis=0).T.astype(BF16)

    group_keys = min(seq_len, BOUNDED_GROUP_KEYS)
    n_sub = group_keys // KEY_SUBTILE

    def bounded_tile(tile, carry):
        load_queries(tile, 0)
        acc_ref[...] = jnp.zeros_like(acc_ref)

        def group(j, carry):
            def keys(t):
                return pl.ds(pl.multiple_of(j * group_keys + t * KEY_SUBTILE, KEY_SUBTILE), KEY_SUBTILE)
            pv = [None] * GROUP
            psum = [None] * GROUP
            s_prev = [None] * GROUP
            for t in range(n_sub + 1):
                if t < n_sub:
                    kt = k_ref[keys(t), :]
                if t > 0:
                    vt = vt_ref[:, keys(t - 1)]
                for h in range(GROUP):
                    s_cur = _dot(kt, qpad_ref[0, h]) if t < n_sub else None
                    if t > 0:
                        p = jnp.exp2(s_prev[h])
                        d = _dot(vt, p.astype(BF16))
                        ps = jnp.sum(p.reshape(KEY_SUBTILE // SUBLANES, SUBLANES, tq), axis=0)
                        pv[h] = d if pv[h] is None else pv[h] + d
                        psum[h] = ps if psum[h] is None else psum[h] + ps
                    s_prev[h] = s_cur
            for h in range(GROUP):
                acc_ref[h, :HEAD_DIM, :] += pv[h]
                acc_ref[h, HEAD_DIM:HEAD_DIM + SUBLANES, :] += psum[h]
            return carry

        lax.fori_loop(0, seq_len // group_keys, group, 0)
        write_tile(tile, partial_sums=True)
        return carry

    bounded = bound_ref[0] <= SAFE_SCORE_BOUND

    @pl.when(bounded)
    def _():
        lax.fori_loop(0, n_tiles, bounded_tile, 0)

    @pl.when(jnp.logical_not(bounded))
    def _():
        load_queries(0, 0)
        _, m_chunk = step(0, 0, bufs[0], None, None, None, None)
        lax.fori_loop(0, n_tiles, tile_body, m_chunk)


def _attention(score_bound, qt, k, vt, tq, tk, unroll, tiles_per_step):
    bsz, _, seq_len = qt.shape
    gw = GROUP * HEAD_DIM
    tqb = tq * tiles_per_step
    assert seq_len % tqb == 0
    return pl.pallas_call(
        functools.partial(_attention_kernel, tq=tq, tk=tk, unroll=unroll),
        grid=(bsz, N_KV_HEADS, seq_len // tqb),
        in_specs=[
            pl.BlockSpec(memory_space=pltpu.SMEM),
            pl.BlockSpec((None, gw, tqb), lambda b, g, i: (b, g, i)),
            pl.BlockSpec((None, seq_len, KV_W), lambda b, g, i: (b, 0, 0)),
            pl.BlockSpec((None, HEAD_DIM, seq_len), lambda b, g, i: (b, g, 0)),
        ],
        out_specs=pl.BlockSpec((None, tqb, gw), lambda b, g, i: (b, i, g)),
        out_shape=jax.ShapeDtypeStruct((bsz, seq_len, ATTN_W), BF16),
        scratch_shapes=[pltpu.VMEM((2, GROUP, KV_W, tq), BF16),
                        pltpu.VMEM((GROUP, tk, tq), F32),
                        pltpu.VMEM((GROUP, tk, tq), F32),
                        pltpu.VMEM((GROUP, HEAD_DIM + SUM_ROWS, tq), F32)],
        compiler_params=_params(("parallel", "parallel", "parallel")),
        name="gqa_attention",
    )(score_bound, qt, k, vt)


def _mem_kv_kernel(mem_ref, wk_ref, wv_ref, kt_ref, v_ref):
    mb = mem_ref[...].astype(BF16)
    kt_ref[...] = _dot(mb, wk_ref[...]).T.astype(BF16)
    v_ref[...] = _dot(mb, wv_ref[...]).astype(BF16)


def _mem_kv(mem2d, w_k, w_v):
    rows = mem2d.shape[0]
    blk = pl.BlockSpec((MEM_TOKENS, D_MODEL), lambda i: (i, 0))
    blk_t = pl.BlockSpec((D_MODEL, MEM_TOKENS), lambda i: (i, 0))
    return pl.pallas_call(
        _mem_kv_kernel,
        grid=(rows // MEM_TOKENS,),
        in_specs=[blk, _resident((D_MODEL, D_MODEL)), _resident((D_MODEL, D_MODEL))],
        out_specs=[blk_t, blk],
        out_shape=[jax.ShapeDtypeStruct((rows // MEM_TOKENS * D_MODEL, MEM_TOKENS), BF16),
                   jax.ShapeDtypeStruct((rows, D_MODEL), BF16)],
        compiler_params=_params(("parallel",)),
        name="mem_kv",
    )(mem2d, w_k, w_v)


MIX_ROWS = 512
MIX_PARTS = 2


def _mix_xattn_kernel(x_ref, o_ref, yr_ref, yi_ref, m2_ref, wg_ref, wab_ref, wfb_ref, wmix_ref,
                      ln1g_ref, ln1b_ref, wq_ref, kmt_ref, vm_ref, wo_ref, ln2g_ref, ln2b_ref,
                      y_ref, f_scr, *, alpha):
    n, fast = x_ref.shape[0], x_ref.shape[1]
    tm = n * fast
    m2 = m2_ref[...]
    for r in range(fast):
        plane = lambda ref: jnp.concatenate([ref[gi, r] for gi in range(FOURIER_GROUPS)], axis=-1)
        y2 = jnp.concatenate([plane(yr_ref), plane(yi_ref)], axis=0).astype(BF16)
        f = _dot(m2, y2)
        for gi in range(FOURIER_GROUPS):
            f_scr.at[gi][pl.ds(r, n, stride=fast), :] = f[:, gi * FOURIER_GROUP_W:(gi + 1) * FOURIER_GROUP_W]

    pass_rows = min(MIX_ROWS, tm)

    n_parts = MIX_PARTS if pass_rows % (MIX_PARTS * fast * SUBLANES) == 0 else 1
    part = pass_rows // n_parts

    def dense(i):
        def rows(g):
            return pl.ds(pl.multiple_of(i * pass_rows + g * part, part), part)

        def merge(g):
            slow = pl.ds(pl.multiple_of((i * pass_rows + g * part) // fast, part // fast), part // fast)
            x = x_ref[slow].reshape(part, D_MODEL)
            gates = _dot(x.astype(BF16), wg_ref[...])
            y_attn = _dot(o_ref[rows(g), :], wab_ref[...])
            f = jnp.concatenate([f_scr[gi, rows(g), :] for gi in range(FOURIER_GROUPS)], axis=-1)
            y_four = _dot(f.astype(BF16), wfb_ref[...])
            merged = (jax.nn.sigmoid(gates[:, :D_MODEL]) * y_attn
                      + jax.nn.sigmoid(gates[:, D_MODEL:]) * y_four)
            return _layernorm(alpha * x + _dot(merged.astype(BF16), wmix_ref[...]), ln1g_ref[...], ln1b_ref[...])

        def cross_attend(x1):
            q = (_dot(x1.astype(BF16), wq_ref[...]) * (MEM_HEAD_DIM ** -0.5 * LOG2E)).astype(BF16)
            heads = []
            for h in range(MEM_HEADS):
                c = h * MEM_HEAD_DIM
                s = _dot(q[:, c:c + MEM_HEAD_DIM], kmt_ref[c:c + MEM_HEAD_DIM, :])
                p = jnp.exp2(s - jnp.max(s, axis=-1, keepdims=True))
                l = jnp.sum(p, axis=-1, keepdims=True)
                heads.append(_dot(p.astype(BF16), vm_ref[:, c:c + MEM_HEAD_DIM]) / l)
            return jnp.concatenate(heads, axis=-1).astype(BF16)

        x1 = [merge(g) for g in range(n_parts)]
        ctx = [cross_attend(x1[g]) for g in range(n_parts)]
        for g in range(n_parts):
            y_ref[rows(g), :] = _layernorm(alpha * x1[g] + _dot(ctx[g], wo_ref[...]), ln2g_ref[...], ln2b_ref[...])

    assert tm % pass_rows == 0
    if tm == pass_rows:
        dense(0)
    else:
        def body(i, carry):
            dense(i)
            return carry
        lax.fori_loop(0, tm // pass_rows, body, 0)


def _mix_xattn(x4, fast, o2d, yr, yi, m2, wg, wab, wfb, wmix, ln1g, ln1b, wq, km, vm, wo, ln2g, ln2b, alpha):
    bsz, n, _, _ = x4.shape
    tm, tiles = n * fast, n // fast
    tok = lambda w: pl.BlockSpec((tm, w), lambda b, j: (b * tiles + j, 0))
    plane = pl.BlockSpec((None, FOURIER_GROUPS, fast, n, FOURIER_GROUP_W), lambda b, j: (b, 0, j, 0, 0))
    mem = pl.BlockSpec((MEM_TOKENS, D_MODEL), lambda b, j: (b, 0))
    mem_t = pl.BlockSpec((D_MODEL, MEM_TOKENS), lambda b, j: (b, 0))
    vec = _resident((1, D_MODEL))
    return pl.pallas_call(
        functools.partial(_mix_xattn_kernel, alpha=alpha),
        grid=(bsz, tiles),
        in_specs=[_tile_view(n, fast, D_MODEL), tok(ATTN_W), plane, plane, _resident((n, 2 * n)),
                  _resident((D_MODEL, 2 * D_MODEL)), _resident((ATTN_W, D_MODEL)),
                  _resident((FOURIER_W, D_MODEL)), _resident((D_MODEL, D_MODEL)), vec, vec,
                  _resident((D_MODEL, D_MODEL)), mem_t, mem, _resident((D_MODEL, D_MODEL)), vec, vec],
        out_specs=tok(D_MODEL),
        out_shape=jax.ShapeDtypeStruct((bsz * n * n, D_MODEL), F32),
        scratch_shapes=[pltpu.VMEM((FOURIER_GROUPS, tm, FOURIER_GROUP_W), F32)],
        compiler_params=_params(("parallel", "parallel")),
        name="mix_xattn",
    )(x4, o2d, yr, yi, m2, wg, wab, wfb, wmix, ln1g, ln1b, wq, km, vm, wo, ln2g, ln2b)


FF_CHUNK = 1024


def _mlp_kernel(x_ref, wup_ref, wdown_ref, g_ref, b_ref, y_ref, *, alpha):
    x = x_ref[...]
    xb = x.astype(BF16)
    y = alpha * x
    for c in range(0, D_FF, FF_CHUNK):
        h = jnp.maximum(_dot(xb, wup_ref[:, c:c + FF_CHUNK]), 0.0)
        y = y + _dot((h * h).astype(BF16), wdown_ref[c:c + FF_CHUNK, :])
    y_ref[...] = _layernorm(y, g_ref[...], b_ref[...]).reshape(y_ref.shape)


def _mlp(x2d, bsz, n, fast, wup, wdown, g, b, alpha):
    tm, tiles = n * fast, n // fast
    vec = _resident((1, D_MODEL))
    return pl.pallas_call(
        functools.partial(_mlp_kernel, alpha=alpha),
        grid=(bsz, tiles),
        in_specs=[pl.BlockSpec((tm, D_MODEL), lambda b, j: (b * tiles + j, 0)),
                  _resident((D_MODEL, D_FF)), _resident((D_FF, D_MODEL)), vec, vec],
        out_specs=_tile_view(n, fast, D_MODEL),
        out_shape=jax.ShapeDtypeStruct((bsz, n, n, D_MODEL), F32),
        compiler_params=_params(("parallel", "parallel")),
        name="mlp",
    )(x2d, wup, wdown, g, b)


def _tile(n, want):
    t = min(n, want)
    assert n % t == 0
    return t


def _layer(x, mem, w, alpha):
    bsz, seq_len, _ = x.shape
    n, fast = _grid_side(seq_len)
    x4 = x.reshape(bsz, n, n, D_MODEL)
    m1, m2 = _dft_matrices(n)
    qt, k, vt, yr, yi = _in_projection(x4, fast, w["qkvu"], w["qk_gain"], m1)
    tq = _tile(seq_len, 512)
    o = _attention(w["score_bound"], qt, k.reshape(bsz, seq_len, KV_W), vt, tq, _tile(seq_len, 512),
                   unroll=4 if seq_len % 8192 == 0 else 2, tiles_per_step=_tile(seq_len // tq, 16))
    km, vm = _mem_kv(mem.reshape(bsz * MEM_TOKENS, D_MODEL), w["mem_k"], w["mem_v"])
    x2 = _mix_xattn(x4, fast, o.reshape(bsz * seq_len, ATTN_W), yr, yi, m2, w["gate"], w["attn_branch"],
                    w["fourier_branch"], w["mix_out"], w["ln1_g"], w["ln1_b"], w["mem_q"], km, vm,
                    w["mem_o"], w["ln2_g"], w["ln2_b"], alpha)
    y = _mlp(x2, bsz, n, fast, w["up"], w["down"], w["ln3_g"], w["ln3_b"], alpha)
    return y.reshape(bsz, seq_len, D_MODEL)


def kernel(x_prompt, x_sample, mem_prompt, mem_sample, w_in, q_norm, k_norm, w_attn_branch, w_fourier_branch, w_mix_out, ln1_g, ln1_b, w_mem_q, w_mem_k, w_mem_v, w_mem_o, ln2_g, ln2_b, w_up, w_down, ln3_g, ln3_b):
    depth = w_in.shape[0]
    alpha = float((2 * depth) ** 0.25)
    y_prompt, y_sample = x_prompt, x_sample
    for l in range(depth):
        row = lambda v: v[l].reshape(1, -1).astype(F32)
        w = {
            "qkvu": w_in[l, :, :QKVU_W].astype(BF16),
            "gate": w_in[l, :, QKVU_W:].astype(BF16),
            "qk_gain": jnp.concatenate([jnp.tile(q_norm[l] * (HEAD_DIM ** -0.5 * LOG2E), N_HEADS),
                                        jnp.tile(k_norm[l], N_KV_HEADS)]).reshape(1, -1).astype(F32),
            "score_bound": (HEAD_DIM ** 0.5 * LOG2E * BOUND_SLACK * jnp.max(jnp.abs(q_norm[l]))
                            * jnp.max(jnp.abs(k_norm[l]))).reshape(1).astype(F32),
            "attn_branch": w_attn_branch[l].astype(BF16),
            "fourier_branch": w_fourier_branch[l].astype(BF16),
            "mix_out": w_mix_out[l].astype(BF16),
            "ln1_g": row(ln1_g), "ln1_b": row(ln1_b),
            "mem_q": w_mem_q[l].astype(BF16),
            "mem_k": w_mem_k[l].astype(BF16), "mem_v": w_mem_v[l].astype(BF16),
            "mem_o": w_mem_o[l].astype(BF16),
            "ln2_g": row(ln2_g), "ln2_b": row(ln2_b),
            "up": w_up[l].astype(BF16), "down": w_down[l].astype(BF16),
            "ln3_g": row(ln3_g), "ln3_b": row(ln3_b),
        }
        y_prompt = _layer(y_prompt, mem_prompt, w, alpha)
        y_sample = _layer(y_sample, mem_sample, w, alpha)
    return (y_prompt, y_sample)
```
